```python
import math
import jax, jax.numpy as jnp
from jax import lax
import numpy as np

D_MODEL = 2048
BATCH = 1
SEQ = 8192
DEPTH = 4

N_MIXERS = 2
BLOCK = 128
A_HEADS = 32
A_KV_HEADS = 4
A_HEAD_DIM = 64
A_GROUP = A_HEADS // A_KV_HEADS
WINDOW = 128
A_QKV_DIM = (A_HEADS + 2 * A_KV_HEADS) * A_HEAD_DIM
B_HEADS = 16
B_HEAD_DIM = 64
B_V_DIM = 2 * B_HEAD_DIM
B_QKV_DIM = B_HEADS * (2 * B_HEAD_DIM + 2 * B_HEAD_DIM + B_V_DIM)
P_HEADS = 8
N_KEYS = 128
N_EXPERTS = N_KEYS * N_KEYS
P_TOPK = 16
P_KEY_DIM = 256
P_HALF = P_KEY_DIM // 2
P_CHUNK = 128

LN_EPS = 1e-5
NEG = -1e30
ALPHA = (2.0 * DEPTH) ** 0.25
BETA = (8.0 * DEPTH) ** -0.25
N_A_LAYERS = (DEPTH + 1) // 2
N_B_LAYERS = DEPTH // 2

kernel_name = "hybrid_swa_sink_diffattn_peer_deepnorm"


def alibi_slopes(n):
    return jnp.asarray((2.0 ** (-8.0 * np.arange(1, n + 1) / n)).astype(np.float32))


def layer_norm(x, g, b):
    xf = x.astype(jnp.float32)
    mu = jnp.mean(xf, axis=-1, keepdims=True)
    var = jnp.mean(jnp.square(xf - mu), axis=-1, keepdims=True)
    y = (xf - mu) * lax.rsqrt(var + LN_EPS)
    return (y * g.astype(jnp.float32) + b.astype(jnp.float32)).astype(x.dtype)


def sliding_window_gqa(x, w_qkv, sinks, w_o):
    B, S, _ = x.shape
    nb = S // BLOCK
    d = A_HEAD_DIM
    qkv = x @ w_qkv
    q, k, v = jnp.split(qkv, [A_HEADS * d, (A_HEADS + A_KV_HEADS) * d], axis=-1)
    q = q.reshape(B, nb, BLOCK, A_KV_HEADS, A_GROUP, d)
    k = k.reshape(B, S, A_KV_HEADS, d)
    v = v.reshape(B, S, A_KV_HEADS, d)
    pad = jnp.zeros((B, BLOCK, A_KV_HEADS, d), x.dtype)
    kp = jnp.concatenate([pad, k], axis=1).reshape(B, nb + 1, BLOCK, A_KV_HEADS, d)
    vp = jnp.concatenate([pad, v], axis=1).reshape(B, nb + 1, BLOCK, A_KV_HEADS, d)
    k_band = jnp.concatenate([kp[:, :-1], kp[:, 1:]], axis=2)
    v_band = jnp.concatenate([vp[:, :-1], vp[:, 1:]], axis=2)

    scores = jnp.einsum('bnqhgd,bnkhd->bnhgqk', q, k_band).astype(jnp.float32) * (d ** -0.5)
    qpos = jnp.arange(BLOCK)[:, None] + BLOCK
    kpos = jnp.arange(2 * BLOCK)[None, :]
    dist = qpos - kpos
    blk = jnp.arange(nb)[:, None, None]
    key_abs = (blk - 1) * BLOCK + kpos[None]
    valid = (dist[None] >= 0) & (dist[None] < WINDOW) & (key_abs >= 0)
    slopes = alibi_slopes(A_HEADS).reshape(A_KV_HEADS, A_GROUP)
    scores = scores - slopes[:, :, None, None] * dist.astype(jnp.float32)
    scores = jnp.where(valid[None, :, None, None], scores, NEG)

    sink = sinks.astype(jnp.float32).reshape(A_KV_HEADS, A_GROUP)[:, :, None, None]
    m = jnp.maximum(jnp.max(scores, axis=-1, keepdims=True), sink)
    p = jnp.exp(scores - m)
    denom = jnp.sum(p, axis=-1, keepdims=True) + jnp.exp(sink - m)
    probs = (p / denom).astype(x.dtype)
    o = jnp.einsum('bnhgqk,bnkhd->bnqhgd', probs, v_band)
    return o.reshape(B, S, A_HEADS * d) @ w_o


def differential_attention(x, w_qkv, lq1, lk1, lq2, lk2, subln_g, w_o, lambda_init):
    B, S, _ = x.shape
    nb = S // BLOCK
    d = B_HEAD_DIM
    qkv = x @ w_qkv
    q, k, v = jnp.split(qkv, 3, axis=-1)
    q = q.reshape(B, S, B_HEADS, 2, d)
    k = k.reshape(B, S, B_HEADS, 2, d)
    v = v.reshape(B, S, B_HEADS, B_V_DIM)
    lam = (jnp.exp(jnp.sum(lq1.astype(jnp.float32) * lk1.astype(jnp.float32)))
           - jnp.exp(jnp.sum(lq2.astype(jnp.float32) * lk2.astype(jnp.float32)))
           + lambda_init)
    slopes = alibi_slopes(B_HEADS)[None, :, None, None, None]
    kpos = jnp.arange(S)

    def block(i):
        qb = lax.dynamic_slice_in_dim(q, i * BLOCK, BLOCK, axis=1)
        s = jnp.einsum('bqhcd,bkhcd->bhcqk', qb, k).astype(jnp.float32) * (d ** -0.5)
        qpos = i * BLOCK + jnp.arange(BLOCK)
        dist = qpos[:, None] - kpos[None, :]
        s = s - slopes * dist.astype(jnp.float32)
        s = jnp.where(dist >= 0, s, NEG)
        p = jax.nn.softmax(s, axis=-1)
        pdiff = (p[:, :, 0] - lam * p[:, :, 1]).astype(x.dtype)
        return jnp.einsum('bhqk,bkhe->bqhe', pdiff, v)

    o = lax.map(block, jnp.arange(nb))
    o = o.transpose(1, 0, 2, 3, 4).reshape(B, S, B_HEADS, B_V_DIM)
    of = o.astype(jnp.float32)
    of = of * lax.rsqrt(jnp.mean(jnp.square(of), axis=-1, keepdims=True) + LN_EPS)
    of = of * subln_g.astype(jnp.float32) * (1.0 - lambda_init)
    return of.astype(x.dtype).reshape(B, S, B_HEADS * B_V_DIM) @ w_o


def peer_ffn(x, w_q, subkeys, u, v):
    B, S, D = x.shape
    xt = x.reshape(B * S // P_CHUNK, P_CHUNK, D)

    def chunk(xc):
        T = xc.shape[0]
        q = (xc @ w_q).reshape(T, P_HEADS, 2, P_HALF)
        sc = jnp.einsum('thcd,hckd->thck', q, subkeys).astype(jnp.float32)
        s_top, i_top = lax.top_k(sc, P_TOPK)
        cand = s_top[:, :, 0, :, None] + s_top[:, :, 1, None, :]
        cand_idx = i_top[:, :, 0, :, None] * N_KEYS + i_top[:, :, 1, None, :]
        cand = cand.reshape(T, P_HEADS, P_TOPK * P_TOPK)
        cand_idx = cand_idx.reshape(T, P_HEADS, P_TOPK * P_TOPK)
        best, pos = lax.top_k(cand, P_TOPK)
        experts = jnp.take_along_axis(cand_idx, pos, axis=-1)
        g = jax.nn.softmax(best, axis=-1).reshape(T, P_HEADS * P_TOPK)
        e = experts.reshape(T, P_HEADS * P_TOPK)
        ue = u[e]
        ve = v[e]
        act = jax.nn.gelu(jnp.einsum('ted,td->te', ue, xc).astype(jnp.float32), approximate=False)
        h = (g * act).astype(x.dtype)
        return jnp.einsum('te,ted->td', h, ve)

    return lax.map(chunk, xt).reshape(B, S, D)


def setup_inputs(seed: int = 0) -> dict:
    key = jax.random.key(seed)
    ks = jax.random.split(key, 20)
    f32 = jnp.float32
    D = D_MODEL
    nrm = lambda k, shape, scale: jax.random.normal(k, shape, f32) * scale
    return {
        'x': nrm(ks[0], (BATCH, SEQ, D), 1.0),
        'a_w_qkv': nrm(ks[1], (N_A_LAYERS, D, A_QKV_DIM), D ** -0.5),
        'a_sinks': nrm(ks[2], (N_A_LAYERS, A_HEADS), 0.5),
        'a_w_o': nrm(ks[3], (N_A_LAYERS, A_HEADS * A_HEAD_DIM, D), BETA * (A_HEADS * A_HEAD_DIM) ** -0.5),
        'b_w_qkv': nrm(ks[4], (N_B_LAYERS, D, B_QKV_DIM), D ** -0.5),
        'b_lambda_q1': nrm(ks[5], (N_B_LAYERS, B_HEAD_DIM), 0.1),
        'b_lambda_k1': nrm(ks[6], (N_B_LAYERS, B_HEAD_DIM), 0.1),
        'b_lambda_q2': nrm(ks[7], (N_B_LAYERS, B_HEAD_DIM), 0.1),
        'b_lambda_k2': nrm(ks[8], (N_B_LAYERS, B_HEAD_DIM), 0.1),
        'b_subln_g': 1.0 + nrm(ks[9], (N_B_LAYERS, B_V_DIM), 0.02),
        'b_w_o': nrm(ks[10], (N_B_LAYERS, B_HEADS * B_V_DIM, D), BETA * (B_HEADS * B_V_DIM) ** -0.5),
        'ln1_g': 1.0 + nrm(ks[11], (DEPTH, D), 0.02),
        'ln1_b': nrm(ks[12], (DEPTH, D), 0.02),
        'ln2_g': 1.0 + nrm(ks[13], (DEPTH, D), 0.02),
        'ln2_b': nrm(ks[14], (DEPTH, D), 0.02),
        'peer_w_q': nrm(ks[15], (DEPTH, D, P_HEADS * P_KEY_DIM), D ** -0.5),
        'peer_subkeys': nrm(ks[16], (DEPTH, P_HEADS, 2, N_KEYS, P_HALF), P_HALF ** -0.5),
        'peer_u': nrm(ks[17], (DEPTH, N_EXPERTS, D), D ** -0.5),
        'peer_v': nrm(ks[18], (DEPTH, N_EXPERTS, D), BETA * P_HEADS ** -0.5),
    }


def reference(x, a_w_qkv, a_sinks, a_w_o, b_w_qkv, b_lambda_q1, b_lambda_k1, b_lambda_q2,
              b_lambda_k2, b_subln_g, b_w_o, ln1_g, ln1_b, ln2_g, ln2_b,
              peer_w_q, peer_subkeys, peer_u, peer_v):
    for i in range(DEPTH):
        j = i // N_MIXERS
        if i % N_MIXERS == 0:
            mix = sliding_window_gqa(x, a_w_qkv[j], a_sinks[j], a_w_o[j])
        else:
            lambda_init = 0.8 - 0.6 * math.exp(-0.3 * i)
            mix = differential_attention(x, b_w_qkv[j], b_lambda_q1[j], b_lambda_k1[j],
                                         b_lambda_q2[j], b_lambda_k2[j], b_subln_g[j],
                                         b_w_o[j], lambda_init)
        x = layer_norm(ALPHA * x + mix, ln1_g[i], ln1_b[i])
        ffn = peer_ffn(x, peer_w_q[i], peer_subkeys[i], peer_u[i], peer_v[i])
        x = layer_norm(ALPHA * x + ffn, ln2_g[i], ln2_b[i])
    return x
```

```python
import functools

import jax
import jax.numpy as jnp
import numpy as np
from jax import lax
from jax.experimental import pallas as pl
from jax.experimental.pallas import tpu as pltpu

F32 = jnp.float32
BF16 = jnp.bfloat16

DEPTH = 4
A_HEADS, A_KV_HEADS, A_HEAD_DIM, WINDOW = 32, 4, 64, 128
A_GROUP = A_HEADS // A_KV_HEADS
B_HEADS, B_HEAD_DIM = 16, 64
B_V_DIM = 2 * B_HEAD_DIM
P_HEADS, N_KEYS, P_TOPK, P_HALF = 8, 128, 16, 128
LN_EPS = 1e-5
NEG = -1e30
ALPHA = (2.0 * DEPTH) ** 0.25

LANES = 128
V7X_VMEM_LIMIT_BYTES = 56 * 1024 * 1024


def _tile(n, pref):
    t = min(n, pref)
    assert n % t == 0, (n, t)
    return t


def _params(*sem):
    return pltpu.CompilerParams(dimension_semantics=sem, vmem_limit_bytes=V7X_VMEM_LIMIT_BYTES)


def _alibi_slopes(n):
    return jnp.asarray((2.0 ** (-8.0 * np.arange(1, n + 1) / n)).astype(np.float32))


def _mm_kernel(a_ref, b_ref, o_ref):
    o_ref[...] = jnp.dot(a_ref[...], b_ref[...], preferred_element_type=F32).astype(o_ref.dtype)


def _matmul(a, b, out_dtype, tm=1024, tn=512):
    m, k = a.shape
    n = b.shape[1]
    tm, tn = _tile(m, tm), _tile(n, tn)
    return pl.pallas_call(
        _mm_kernel,
        grid=(m // tm, n // tn),
        in_specs=[pl.BlockSpec((tm, k), lambda i, j: (i, 0)),
                  pl.BlockSpec((k, tn), lambda i, j: (0, j))],
        out_specs=pl.BlockSpec((tm, tn), lambda i, j: (i, j)),
        out_shape=jax.ShapeDtypeStruct((m, n), out_dtype),
        compiler_params=_params("parallel", "parallel"),
        name="matmul",
    )(a, b)


def _layer_norm_rows(z, g, b):
    mu = jnp.mean(z, axis=-1, keepdims=True)
    zc = z - mu
    var = jnp.mean(zc * zc, axis=-1, keepdims=True)
    return zc * lax.rsqrt(var + LN_EPS) * g + b


def _proj_ln_kernel(a_ref, w_ref, x_ref, g_ref, b_ref, of_ref, ob_ref):
    y = jnp.dot(a_ref[...], w_ref[...], preferred_element_type=F32)
    out = _layer_norm_rows(ALPHA * x_ref[...] + y, g_ref[...], b_ref[...])
    of_ref[...] = out
    ob_ref[...] = out.astype(BF16)


def _proj_residual_ln(a, w, x, g, b, tm=256):
    s, k = a.shape
    d = w.shape[1]
    tm = _tile(s, tm)
    row = lambda i: (i, 0)
    fixed = lambda i: (0, 0)
    return pl.pallas_call(
        _proj_ln_kernel,
        grid=(s // tm,),
        in_specs=[pl.BlockSpec((tm, k), row), pl.BlockSpec((k, d), fixed),
                  pl.BlockSpec((tm, d), row), pl.BlockSpec((1, d), fixed),
                  pl.BlockSpec((1, d), fixed)],
        out_specs=[pl.BlockSpec((tm, d), row), pl.BlockSpec((tm, d), row)],
        out_shape=[jax.ShapeDtypeStruct((s, d), F32), jax.ShapeDtypeStruct((s, d), BF16)],
        compiler_params=_params("parallel"),
        name="proj_residual_ln",
    )(a, w, x, g.reshape(1, d), b.reshape(1, d))


def _add_ln_kernel(y_ref, x_ref, g_ref, b_ref, of_ref, ob_ref):
    out = _layer_norm_rows(ALPHA * x_ref[...] + y_ref[...], g_ref[...], b_ref[...])
    of_ref[...] = out
    ob_ref[...] = out.astype(BF16)


def _residual_ln(y, x, g, b, tm=512):
    s, d = x.shape
    tm = _tile(s, tm)
    row = lambda i: (i, 0)
    fixed = lambda i: (0, 0)
    return pl.pallas_call(
        _add_ln_kernel,
        grid=(s // tm,),
        in_specs=[pl.BlockSpec((tm, d), row), pl.BlockSpec((tm, d), row),
                  pl.BlockSpec((1, d), fixed), pl.BlockSpec((1, d), fixed)],
        out_specs=[pl.BlockSpec((tm, d), row), pl.BlockSpec((tm, d), row)],
        out_shape=[jax.ShapeDtypeStruct((s, d), F32), jax.ShapeDtypeStruct((s, d), BF16)],
        compiler_params=_params("parallel"),
        name="residual_ln",
    )(y, x, g.reshape(1, d), b.reshape(1, d))


A_Q_COLS = A_HEADS * A_HEAD_DIM
A_PAIR_COLS = 2 * A_HEAD_DIM
A_K_BLOCK0 = A_Q_COLS // A_PAIR_COLS
A_V_BLOCK0 = A_K_BLOCK0 + A_KV_HEADS
A_GROUP_COLS = A_GROUP * A_HEAD_DIM


def _swa_weight(w_qkv):
    d = w_qkv.shape[0]
    wq = w_qkv[:, :A_Q_COLS]
    wk = w_qkv[:, A_Q_COLS:A_Q_COLS + A_KV_HEADS * A_HEAD_DIM].reshape(d, A_KV_HEADS, A_HEAD_DIM)
    wv = w_qkv[:, A_Q_COLS + A_KV_HEADS * A_HEAD_DIM:].reshape(d, A_KV_HEADS, A_HEAD_DIM)
    dup = lambda w: jnp.concatenate([w, w], axis=-1).reshape(d, A_KV_HEADS * A_PAIR_COLS)
    return jnp.concatenate([wq, dup(wk), dup(wv)], axis=1).astype(BF16)


def _half_masks(dtype):
    lane = lax.broadcasted_iota(jnp.int32, (1, LANES), 1)
    lo = (lane < LANES // 2).astype(F32)
    return lo.astype(dtype), (1.0 - lo).astype(dtype)


def _swa_kernel(q_ref, kp_ref, kc_ref, vp_ref, vc_ref, sink_ref, slope_ref, o_ref):
    i = pl.program_id(0)
    g = pl.program_id(1)
    blk = WINDOW
    kband = jnp.concatenate([kp_ref[...], kc_ref[...]], axis=0)
    vband = jnp.concatenate([vp_ref[...], vc_ref[...]], axis=0)
    row = lax.broadcasted_iota(jnp.int32, (2 * blk, 2 * blk), 0)
    col = lax.broadcasted_iota(jnp.int32, (2 * blk, 2 * blk), 1)
    dist = jnp.bitwise_and(row, blk - 1) + blk - col
    valid = (dist >= 0) & (dist < WINDOW) & ((col >= blk) | (i > 0))
    distf = dist.astype(F32)
    top = row < blk
    top_col = top[:, :1]
    mlo, mhi = _half_masks(BF16)
    lo_f = lax.broadcasted_iota(jnp.int32, (blk, LANES), 1) < LANES // 2
    for p in range(A_GROUP // 2):
        h0 = g * A_GROUP + 2 * p
        q2 = q_ref[:, p * LANES:(p + 1) * LANES]
        lhs = jnp.concatenate([q2 * mlo, q2 * mhi], axis=0)
        s = lax.dot_general(lhs, kband, (((1,), (1,)), ((), ())),
                            preferred_element_type=F32) * (A_HEAD_DIM ** -0.5)
        slope = jnp.where(top, slope_ref[h0], slope_ref[h0 + 1])
        sink = jnp.where(top_col, sink_ref[h0], sink_ref[h0 + 1])
        s = s - slope * distf
        s = jnp.where(valid, s, NEG)
        m = jnp.maximum(jnp.max(s, axis=-1, keepdims=True), sink)
        pexp = jnp.exp(s - m)
        denom = jnp.sum(pexp, axis=-1, keepdims=True) + jnp.exp(sink - m)
        probs = (pexp / denom).astype(BF16)
        o2 = jnp.dot(probs, vband, preferred_element_type=F32)
        o_ref[:, p * LANES:(p + 1) * LANES] = jnp.where(lo_f, o2[:blk], o2[blk:]).astype(o_ref.dtype)


def _swa_attention(qkv, sinks):
    s = qkv.shape[0]
    nb = s // WINDOW
    prev = lambda i: jnp.maximum(i - 1, 0)
    smem = pl.BlockSpec(memory_space=pltpu.SMEM)
    return pl.pallas_call(
        _swa_kernel,
        grid=(nb, A_KV_HEADS),
        in_specs=[pl.BlockSpec((WINDOW, A_GROUP_COLS), lambda i, g: (i, g)),
                  pl.BlockSpec((WINDOW, A_PAIR_COLS), lambda i, g: (prev(i), A_K_BLOCK0 + g)),
                  pl.BlockSpec((WINDOW, A_PAIR_COLS), lambda i, g: (i, A_K_BLOCK0 + g)),
                  pl.BlockSpec((WINDOW, A_PAIR_COLS), lambda i, g: (prev(i), A_V_BLOCK0 + g)),
                  pl.BlockSpec((WINDOW, A_PAIR_COLS), lambda i, g: (i, A_V_BLOCK0 + g)),
                  smem, smem],
        out_specs=pl.BlockSpec((WINDOW, A_GROUP_COLS), lambda i, g: (i, g)),
        out_shape=jax.ShapeDtypeStruct((s, A_Q_COLS), BF16),
        compiler_params=_params("parallel", "parallel"),
        name="swa_attention",
    )(qkv, qkv, qkv, qkv, qkv, sinks.astype(F32), _alibi_slopes(A_HEADS))


B_QK_COLS = B_HEADS * 2 * B_HEAD_DIM
B_K_BLOCK0 = B_QK_COLS // B_V_DIM
B_V_BLOCK0 = 2 * B_K_BLOCK0


def _diff_kernel(q_ref, k_ref, v_ref, lq1_ref, lk1_ref, lq2_ref, lk2_ref, g_ref, slope_ref,
                 o_ref, lhs_scr, m_scr, l_scr, acc_scr, *, tq, tk, lambda_init):
    h = pl.program_id(0)
    i = pl.program_id(1)
    j = pl.program_id(2)
    jlast = (i * tq + tq - 1) // tk

    @pl.when(j == 0)
    def _init():
        mlo, mhi = _half_masks(BF16)
        q = q_ref[...]
        lhs_scr[0:tq, :] = q * mlo
        lhs_scr[tq:2 * tq, :] = q * mhi
        m_scr[...] = jnp.full(m_scr.shape, -jnp.inf, F32)
        l_scr[...] = jnp.zeros(l_scr.shape, F32)
        acc_scr[...] = jnp.zeros(acc_scr.shape, F32)

    @pl.when(j <= jlast)
    def _step():
        s = lax.dot_general(lhs_scr[...], k_ref[...], (((1,), (1,)), ((), ())),
                            preferred_element_type=F32) * (B_HEAD_DIM ** -0.5)
        row = lax.broadcasted_iota(jnp.int32, (2 * tq, tk), 0)
        col = lax.broadcasted_iota(jnp.int32, (2 * tq, tk), 1)
        dist = (i * tq + jnp.where(row >= tq, row - tq, row)) - (j * tk + col)
        s = s - slope_ref[h] * dist.astype(F32)
        s = jnp.where(dist >= 0, s, NEG)
        m_old = m_scr[...]
        m_new = jnp.maximum(m_old, jnp.max(s, axis=-1, keepdims=True))
        alpha = jnp.exp(m_old - m_new)
        p = jnp.exp(s - m_new)
        l_scr[...] = alpha * l_scr[...] + jnp.sum(p, axis=-1, keepdims=True)
        acc_scr[...] = alpha * acc_scr[...] + jnp.dot(p.astype(BF16), v_ref[...],
                                                      preferred_element_type=F32)
        m_scr[...] = m_new

    @pl.when(j == jlast)
    def _finish():
        lam = (jnp.exp(jnp.sum(lq1_ref[...] * lk1_ref[...], axis=-1, keepdims=True))
               - jnp.exp(jnp.sum(lq2_ref[...] * lk2_ref[...], axis=-1, keepdims=True))
               + lambda_init)
        o = acc_scr[0:tq, :] / l_scr[0:tq, :] - lam * (acc_scr[tq:2 * tq, :] / l_scr[tq:2 * tq, :])
        o = o * lax.rsqrt(jnp.mean(o * o, axis=-1, keepdims=True) + LN_EPS)
        o = o * g_ref[...] * (1.0 - lambda_init)
        o_ref[...] = o.astype(o_ref.dtype)


def _diff_attention(qkv, lq1, lk1, lq2, lk2, subln_g, lambda_init, tq=256, tk=512):
    s = qkv.shape[0]
    tq, tk = _tile(s, tq), _tile(s, tk)
    kv_blk = lambda i, j: jnp.minimum(j, (i * tq + tq - 1) // tk)
    vec = lambda n: pl.BlockSpec((1, n), lambda h, i, j: (0, 0))
    kern = functools.partial(_diff_kernel, tq=tq, tk=tk, lambda_init=lambda_init)
    return pl.pallas_call(
        kern,
        grid=(B_HEADS, s // tq, s // tk),
        in_specs=[pl.BlockSpec((tq, B_V_DIM), lambda h, i, j: (i, h)),
                  pl.BlockSpec((tk, B_V_DIM), lambda h, i, j: (kv_blk(i, j), B_K_BLOCK0 + h)),
                  pl.BlockSpec((tk, B_V_DIM), lambda h, i, j: (kv_blk(i, j), B_V_BLOCK0 + h)),
                  vec(B_HEAD_DIM), vec(B_HEAD_DIM), vec(B_HEAD_DIM), vec(B_HEAD_DIM),
                  vec(B_V_DIM), pl.BlockSpec(memory_space=pltpu.SMEM)],
        out_specs=pl.BlockSpec((tq, B_V_DIM), lambda h, i, j: (i, h)),
        out_shape=jax.ShapeDtypeStruct((s, B_HEADS * B_V_DIM), BF16),
        scratch_shapes=[pltpu.VMEM((2 * tq, B_V_DIM), BF16), pltpu.VMEM((2 * tq, 1), F32),
                        pltpu.VMEM((2 * tq, 1), F32), pltpu.VMEM((2 * tq, B_V_DIM), F32)],
        compiler_params=_params("parallel", "parallel", "arbitrary"),
        name="diff_attention",
    )(qkv, qkv, qkv, lq1.reshape(1, -1).astype(F32), lk1.reshape(1, -1).astype(F32),
      lq2.reshape(1, -1).astype(F32), lk2.reshape(1, -1).astype(F32),
      subln_g.reshape(1, -1).astype(F32), _alibi_slopes(B_HEADS))


_PAIR_RANKS = [(i, j) for i in range(P_TOPK) for j in range(P_TOPK) if (i + 1) * (j + 1) <= P_TOPK]
_CAND_ROWS = -(-len(_PAIR_RANKS) // 8) * 8


def _top_values(work, count):
    out = []
    for _ in range(count):
        mx = jnp.max(work, axis=0, keepdims=True)
        out.append(mx)
        work = jnp.where(work == mx, -jnp.inf, work)
    return out


def _peer_score_kernel(q_ref, sk_ref, a1_ref, e1_ref, a2_ref, e2_ref, tau_ref, cand_scr):
    tt = q_ref.shape[0]
    cand_scr[...] = jnp.full(cand_scr.shape, -jnp.inf, F32)
    for h in range(P_HEADS):
        st = []
        for c in range(2):
            qs = q_ref[:, (2 * h + c) * P_HALF:(2 * h + c + 1) * P_HALF]
            st.append(lax.dot_general(sk_ref[h, c], qs, (((1,), (1,)), ((), ())),
                                      preferred_element_type=F32))
        a = _top_values(st[0], P_TOPK)
        b = _top_values(st[1], P_TOPK)
        for r, (i, j) in enumerate(_PAIR_RANKS):
            cand_scr[r:r + 1, :] = a[i] + b[j]
        best = _top_values(cand_scr[...], P_TOPK)
        z = jnp.zeros((1, tt), F32)
        for v in best:
            z = z + jnp.exp(v - best[0])
        a1_ref[h] = st[0]
        a2_ref[h] = st[1]
        e1_ref[h] = jnp.exp(st[0] - a[0]) / z
        e2_ref[h] = jnp.exp(st[1] - b[0])
        tau_ref[h:h + 1, :] = best[P_TOPK - 1]


def _peer_scores(q, subkeys, tt=512):
    s = q.shape[0]
    tt = _tile(s, tt)
    big = jax.ShapeDtypeStruct((P_HEADS, N_KEYS, s), F32)
    big_spec = pl.BlockSpec((P_HEADS, N_KEYS, tt), lambda t: (0, 0, t))
    return pl.pallas_call(
        _peer_score_kernel,
        grid=(s // tt,),
        in_specs=[pl.BlockSpec((tt, q.shape[1]), lambda t: (t, 0)),
                  pl.BlockSpec(subkeys.shape, lambda t: (0, 0, 0, 0))],
        out_specs=[big_spec, big_spec, big_spec, big_spec,
                   pl.BlockSpec((P_HEADS, tt), lambda t: (0, t))],
        out_shape=[big, big, big, big, jax.ShapeDtypeStruct((P_HEADS, s), F32)],
        scratch_shapes=[pltpu.VMEM((_CAND_ROWS, tt), F32)],
        compiler_params=_params("parallel"),
        name="peer_scores",
    )(q, subkeys)


def _gelu(a):
    return 0.5 * a * (1.0 + lax.erf(a * np.float32(np.sqrt(0.5))))


def _peer_dense_kernel(x_ref, u_ref, vt_ref, a1_ref, e1_ref, a2_ref, e2_ref, tau_ref,
                       o_ref, h_scr):
    e = pl.program_id(1)
    te = u_ref.shape[0]

    @pl.when(e == 0)
    def _init():
        o_ref[...] = jnp.zeros(o_ref.shape, F32)

    act = lax.dot_general(u_ref[...], x_ref[...], (((1,), (1,)), ((), ())),
                          preferred_element_type=F32)
    for il in range(te // N_KEYS):
        gate = jnp.zeros((N_KEYS, act.shape[1]), F32)
        for h in range(P_HEADS):
            pair = a2_ref[h] + a1_ref[h, il:il + 1, :]
            w = e2_ref[h] * e1_ref[h, il:il + 1, :]
            gate = gate + jnp.where(pair >= tau_ref[h:h + 1, :], w, 0.0)
        rows = slice(il * N_KEYS, (il + 1) * N_KEYS)
        h_scr[rows, :] = (gate * _gelu(act[rows, :])).astype(BF16)
    o_ref[...] += jnp.dot(vt_ref[...], h_scr[...], preferred_element_type=F32)


def _peer_dense(x_bf, u_bf, vt_bf, a1, e1, a2, e2, tau, tt=512, te=1024):
    s, d = x_bf.shape
    n_exp = u_bf.shape[0]
    tt, te = _tile(s, tt), _tile(n_exp, te)
    rows_spec = pl.BlockSpec((P_HEADS, te // N_KEYS, tt), lambda t, e: (0, e, t))
    full_spec = pl.BlockSpec((P_HEADS, N_KEYS, tt), lambda t, e: (0, 0, t))
    return pl.pallas_call(
        _peer_dense_kernel,
        grid=(s // tt, n_exp // te),
        in_specs=[pl.BlockSpec((tt, d), lambda t, e: (t, 0)),
                  pl.BlockSpec((te, d), lambda t, e: (e, 0)),
                  pl.BlockSpec((d, te), lambda t, e: (0, e)),
                  rows_spec, rows_spec, full_spec, full_spec,
                  pl.BlockSpec((P_HEADS, tt), lambda t, e: (0, t))],
        out_specs=pl.BlockSpec((d, tt), lambda t, e: (0, t)),
        out_shape=jax.ShapeDtypeStruct((d, s), F32),
        scratch_shapes=[pltpu.VMEM((te, tt), BF16)],
        compiler_params=_params("parallel", "arbitrary"),
        name="peer_dense",
    )(x_bf, u_bf, vt_bf, a1, e1, a2, e2, tau)


def _peer_ffn(x_bf, w_q, subkeys, u, v):
    q = _matmul(x_bf, w_q.astype(BF16), BF16)
    a1, e1, a2, e2, tau = _peer_scores(q, subkeys.astype(BF16))
    return _peer_dense(x_bf, u.astype(BF16), v.astype(BF16).T, a1, e1, a2, e2, tau).T


def kernel(x, a_w_qkv, a_sinks, a_w_o, b_w_qkv, b_lambda_q1, b_lambda_k1, b_lambda_q2,
           b_lambda_k2, b_subln_g, b_w_o, ln1_g, ln1_b, ln2_g, ln2_b,
           peer_w_q, peer_subkeys, peer_u, peer_v):
    bsz, seq, d = x.shape
    xf = x.reshape(bsz * seq, d).astype(F32)
    assert bsz == 1, "attention kernels index one sequence"
    xb = xf.astype(BF16)
    for i in range(DEPTH):
        j = i // 2
        if i % 2 == 0:
            qkv = _matmul(xb, _swa_weight(a_w_qkv[j]), BF16)
            mix = _swa_attention(qkv, a_sinks[j])
            w_o = a_w_o[j]
        else:
            lambda_init = 0.8 - 0.6 * float(np.exp(-0.3 * i))
            qkv = _matmul(xb, b_w_qkv[j].astype(BF16), BF16)
            mix = _diff_attention(qkv, b_lambda_q1[j], b_lambda_k1[j], b_lambda_q2[j],
                                  b_lambda_k2[j], b_subln_g[j], lambda_init)
            w_o = b_w_o[j]
        xf, xb = _proj_residual_ln(mix, w_o.astype(BF16), xf, ln1_g[i], ln1_b[i])
        ffn = _peer_ffn(xb, peer_w_q[i], peer_subkeys[i], peer_u[i], peer_v[i])
        xf, xb = _residual_ln(ffn, xf, ln2_g[i], ln2_b[i])
    return xf.reshape(bsz, seq, d).astype(x.dtype)
```

```python
import functools

import jax
import jax.numpy as jnp
import numpy as np
from jax import lax
from jax.experimental import pallas as pl
from jax.experimental.pallas import tpu as pltpu

F32 = jnp.float32
BF16 = jnp.bfloat16

DEPTH = 4
A_HEADS, A_KV_HEADS, A_HEAD_DIM, WINDOW = 32, 4, 64, 128
A_GROUP = A_HEADS // A_KV_HEADS
B_HEADS, B_HEAD_DIM = 16, 64
B_V_DIM = 2 * B_HEAD_DIM
P_HEADS, N_KEYS, P_TOPK, P_HALF = 8, 128, 16, 128
LN_EPS = 1e-5
NEG = -1e30
ALPHA = (2.0 * DEPTH) ** 0.25

LANES = 128
V7X_VMEM_LIMIT_BYTES = 56 * 1024 * 1024


def _tile(n, pref):
    t = min(n, pref)
    assert n % t == 0, (n, t)
    return t


def _params(*sem, flags=None):
    return pltpu.CompilerParams(dimension_semantics=sem, vmem_limit_bytes=V7X_VMEM_LIMIT_BYTES,
                                flags=flags)


def _alibi_slopes(n):
    return jnp.asarray((2.0 ** (-8.0 * np.arange(1, n + 1) / n)).astype(np.float32))


def _mm_kernel(a_ref, b_ref, o_ref):
    o_ref[...] = jnp.dot(a_ref[...], b_ref[...], preferred_element_type=F32).astype(o_ref.dtype)


def _matmul(a, b, out_dtype, tm=1024, tn=512):
    m, k = a.shape
    n = b.shape[1]
    tm, tn = _tile(m, tm), _tile(n, tn)
    return pl.pallas_call(
        _mm_kernel,
        grid=(m // tm, n // tn),
        in_specs=[pl.BlockSpec((tm, k), lambda i, j: (i, 0)),
                  pl.BlockSpec((k, tn), lambda i, j: (0, j))],
        out_specs=pl.BlockSpec((tm, tn), lambda i, j: (i, j)),
        out_shape=jax.ShapeDtypeStruct((m, n), out_dtype),
        compiler_params=_params("parallel", "parallel"),
        name="matmul",
    )(a, b)


def _layer_norm_rows(z, g, b):
    mu = jnp.mean(z, axis=-1, keepdims=True)
    zc = z - mu
    var = jnp.mean(zc * zc, axis=-1, keepdims=True)
    return zc * lax.rsqrt(var + LN_EPS) * g + b


def _proj_ln_kernel(a_ref, w_ref, x_ref, g_ref, b_ref, of_ref, ob_ref):
    y = jnp.dot(a_ref[...], w_ref[...], preferred_element_type=F32)
    out = _layer_norm_rows(ALPHA * x_ref[...] + y, g_ref[...], b_ref[...])
    of_ref[...] = out
    ob_ref[...] = out.astype(BF16)


def _proj_residual_ln(a, w, x, g, b, tm=256):
    s, k = a.shape
    d = w.shape[1]
    tm = _tile(s, tm)
    row = lambda i: (i, 0)
    fixed = lambda i: (0, 0)
    return pl.pallas_call(
        _proj_ln_kernel,
        grid=(s // tm,),
        in_specs=[pl.BlockSpec((tm, k), row), pl.BlockSpec((k, d), fixed),
                  pl.BlockSpec((tm, d), row), pl.BlockSpec((1, d), fixed),
                  pl.BlockSpec((1, d), fixed)],
        out_specs=[pl.BlockSpec((tm, d), row), pl.BlockSpec((tm, d), row)],
        out_shape=[jax.ShapeDtypeStruct((s, d), F32), jax.ShapeDtypeStruct((s, d), BF16)],
        compiler_params=_params("parallel"),
        name="proj_residual_ln",
    )(a, w, x, g.reshape(1, d), b.reshape(1, d))


def _add_ln_kernel(y_ref, x_ref, g_ref, b_ref, of_ref, ob_ref):
    out = _layer_norm_rows(ALPHA * x_ref[...] + y_ref[...], g_ref[...], b_ref[...])
    of_ref[...] = out
    ob_ref[...] = out.astype(BF16)


def _residual_ln(y, x, g, b, tm=512):
    s, d = x.shape
    tm = _tile(s, tm)
    row = lambda i: (i, 0)
    fixed = lambda i: (0, 0)
    return pl.pallas_call(
        _add_ln_kernel,
        grid=(s // tm,),
        in_specs=[pl.BlockSpec((tm, d), row), pl.BlockSpec((tm, d), row),
                  pl.BlockSpec((1, d), fixed), pl.BlockSpec((1, d), fixed)],
        out_specs=[pl.BlockSpec((tm, d), row), pl.BlockSpec((tm, d), row)],
        out_shape=[jax.ShapeDtypeStruct((s, d), F32), jax.ShapeDtypeStruct((s, d), BF16)],
        compiler_params=_params("parallel"),
        name="residual_ln",
    )(y, x, g.reshape(1, d), b.reshape(1, d))


A_Q_COLS = A_HEADS * A_HEAD_DIM
A_PAIR_COLS = 2 * A_HEAD_DIM
A_K_BLOCK0 = A_Q_COLS // A_PAIR_COLS
A_V_BLOCK0 = A_K_BLOCK0 + A_KV_HEADS
A_GROUP_COLS = A_GROUP * A_HEAD_DIM


def _swa_weight(w_qkv):
    d = w_qkv.shape[0]
    wq = w_qkv[:, :A_Q_COLS]
    wk = w_qkv[:, A_Q_COLS:A_Q_COLS + A_KV_HEADS * A_HEAD_DIM].reshape(d, A_KV_HEADS, A_HEAD_DIM)
    wv = w_qkv[:, A_Q_COLS + A_KV_HEADS * A_HEAD_DIM:].reshape(d, A_KV_HEADS, A_HEAD_DIM)
    dup = lambda w: jnp.concatenate([w, w], axis=-1).reshape(d, A_KV_HEADS * A_PAIR_COLS)
    return jnp.concatenate([wq, dup(wk), dup(wv)], axis=1).astype(BF16)


def _half_masks(dtype):
    lane = lax.broadcasted_iota(jnp.int32, (1, LANES), 1)
    lo = (lane < LANES // 2).astype(F32)
    return lo.astype(dtype), (1.0 - lo).astype(dtype)


def _swa_kernel(q_ref, kp_ref, kc_ref, vp_ref, vc_ref, sink_ref, slope_ref, o_ref):
    i = pl.program_id(0)
    g = pl.program_id(1)
    blk = WINDOW
    kband = jnp.concatenate([kp_ref[...], kc_ref[...]], axis=0)
    vband = jnp.concatenate([vp_ref[...], vc_ref[...]], axis=0)
    row = lax.broadcasted_iota(jnp.int32, (2 * blk, 2 * blk), 0)
    col = lax.broadcasted_iota(jnp.int32, (2 * blk, 2 * blk), 1)
    dist = jnp.bitwise_and(row, blk - 1) + blk - col
    valid = (dist >= 0) & (dist < WINDOW) & ((col >= blk) | (i > 0))
    distf = dist.astype(F32)
    top = row < blk
    top_col = top[:, :1]
    mlo, mhi = _half_masks(BF16)
    lo_f = lax.broadcasted_iota(jnp.int32, (blk, LANES), 1) < LANES // 2
    for p in range(A_GROUP // 2):
        h0 = g * A_GROUP + 2 * p
        q2 = q_ref[:, p * LANES:(p + 1) * LANES]
        lhs = jnp.concatenate([q2 * mlo, q2 * mhi], axis=0)
        s = lax.dot_general(lhs, kband, (((1,), (1,)), ((), ())),
                            preferred_element_type=F32) * (A_HEAD_DIM ** -0.5)
        slope = jnp.where(top, slope_ref[h0], slope_ref[h0 + 1])
        sink = jnp.where(top_col, sink_ref[h0], sink_ref[h0 + 1])
        s = s - slope * distf
        s = jnp.where(valid, s, NEG)
        m = jnp.maximum(jnp.max(s, axis=-1, keepdims=True), sink)
        pexp = jnp.exp(s - m)
        denom = jnp.sum(pexp, axis=-1, keepdims=True) + jnp.exp(sink - m)
        probs = (pexp / denom).astype(BF16)
        o2 = jnp.dot(probs, vband, preferred_element_type=F32)
        o_ref[:, p * LANES:(p + 1) * LANES] = jnp.where(lo_f, o2[:blk], o2[blk:]).astype(o_ref.dtype)


def _swa_attention(qkv, sinks):
    s = qkv.shape[0]
    nb = s // WINDOW
    prev = lambda i: jnp.maximum(i - 1, 0)
    smem = pl.BlockSpec(memory_space=pltpu.SMEM)
    return pl.pallas_call(
        _swa_kernel,
        grid=(nb, A_KV_HEADS),
        in_specs=[pl.BlockSpec((WINDOW, A_GROUP_COLS), lambda i, g: (i, g)),
                  pl.BlockSpec((WINDOW, A_PAIR_COLS), lambda i, g: (prev(i), A_K_BLOCK0 + g)),
                  pl.BlockSpec((WINDOW, A_PAIR_COLS), lambda i, g: (i, A_K_BLOCK0 + g)),
                  pl.BlockSpec((WINDOW, A_PAIR_COLS), lambda i, g: (prev(i), A_V_BLOCK0 + g)),
                  pl.BlockSpec((WINDOW, A_PAIR_COLS), lambda i, g: (i, A_V_BLOCK0 + g)),
                  smem, smem],
        out_specs=pl.BlockSpec((WINDOW, A_GROUP_COLS), lambda i, g: (i, g)),
        out_shape=jax.ShapeDtypeStruct((s, A_Q_COLS), BF16),
        compiler_params=_params("parallel", "parallel"),
        name="swa_attention",
    )(qkv, qkv, qkv, qkv, qkv, sinks.astype(F32), _alibi_slopes(A_HEADS))


B_QK_COLS = B_HEADS * 2 * B_HEAD_DIM
B_K_BLOCK0 = B_QK_COLS // B_V_DIM
B_V_BLOCK0 = 2 * B_K_BLOCK0


_ALIBI_COLS = 12
_POS_LOW = 256


def _diff_head_tables(slope, lhs_scr, kx_scr, vx_scr, tq):
    sv = jnp.full((1, LANES), slope, F32)
    s1 = sv.astype(BF16).astype(F32)
    s2 = (sv - s1).astype(BF16).astype(F32)
    s3 = (sv - s1 - s2).astype(BF16).astype(F32)
    lane = lax.broadcasted_iota(jnp.int32, (1, LANES), 1)
    third = lambda k: (lane == k) | (lane == k + 3) | (lane == k + 6) | (lane == k + 9)
    piece = jnp.where(third(0), s1, jnp.where(third(1), s2, s3))

    row = lax.broadcasted_iota(jnp.int32, (2 * tq, LANES), 0)
    lane_q = lax.broadcasted_iota(jnp.int32, (2 * tq, LANES), 1)
    r = jnp.where(row >= tq, row - tq, row)
    r_lo = jnp.bitwise_and(r, _POS_LOW - 1).astype(F32)
    r_hi = jnp.where(r >= _POS_LOW, 1.0, 0.0)
    ext = jnp.where(lane_q < 3, r_lo, jnp.where(lane_q < 6, r_hi,
                    jnp.where(lane_q < _ALIBI_COLS, piece, 0.0)))
    lhs_scr[:, LANES:2 * LANES] = ext.astype(BF16)

    c = lax.broadcasted_iota(jnp.int32, (tq, LANES), 0)
    lane_k = lax.broadcasted_iota(jnp.int32, (tq, LANES), 1)
    c_lo = jnp.bitwise_and(c, _POS_LOW - 1).astype(F32)
    c_hi = jnp.where(c >= _POS_LOW, float(_POS_LOW), 0.0)
    kext = jnp.where(lane_k < 3, -piece, jnp.where(lane_k < 6, -float(_POS_LOW) * piece,
                     jnp.where(lane_k < 9, c_lo, jnp.where(lane_k < _ALIBI_COLS, c_hi, 0.0))))
    kx_scr[...] = kext.astype(BF16)
    vx_scr[...] = jnp.where(lane_k == 0, 1.0, 0.0).astype(BF16)


def _diff_kernel(q_ref, k_ref, v_ref, lq1_ref, lk1_ref, lq2_ref, lk2_ref, g_ref, slope_ref,
                 o_ref, lhs_scr, kx_scr, vx_scr, m_scr, acc_scr, s_scr, *, tq, lambda_init):
    h = pl.program_id(0)
    i = pl.program_id(1)
    slope = slope_ref[h]

    @pl.when(i == 0)
    def _head_init():
        _diff_head_tables(slope, lhs_scr, kx_scr, vx_scr, tq)

    mlo, mhi = _half_masks(BF16)
    q = q_ref[...] * (B_HEAD_DIM ** -0.5)
    lhs_scr[0:tq, 0:LANES] = q * mlo
    lhs_scr[tq:2 * tq, 0:LANES] = q * mhi
    m_scr[...] = jnp.full(m_scr.shape, -jnp.inf, F32)
    acc_scr[...] = jnp.zeros(acc_scr.shape, F32)

    def scores(j, slot):
        start = pl.multiple_of(j * tq, tq)
        kx = jnp.concatenate([k_ref[pl.ds(start, tq), :], kx_scr[...]], axis=1)
        s_scr[slot] = lax.dot_general(lhs_scr[...], kx, (((1,), (1,)), ((), ())),
                                      preferred_element_type=F32)

    def accumulate(j, slot, diagonal):
        start = pl.multiple_of(j * tq, tq)
        vx = jnp.concatenate([v_ref[pl.ds(start, tq), :], vx_scr[...]], axis=1)
        for half in range(2):
            rows = slice(half * tq, (half + 1) * tq)
            s = s_scr[slot, rows, :]
            if diagonal:
                row = lax.broadcasted_iota(jnp.int32, (tq, tq), 0)
                col = lax.broadcasted_iota(jnp.int32, (tq, tq), 1)
                s = jnp.where(row >= col, s, NEG)
                off = 0.0
            else:
                off = slope * ((i - j) * tq).astype(F32)
            m_old = m_scr[rows, :]
            m_new = jnp.maximum(m_old, jnp.max(s, axis=-1, keepdims=True) - off)
            p = jnp.exp(s - (m_new + off))
            alpha = jnp.exp(m_old - m_new)
            acc_scr[rows, :] = alpha * acc_scr[rows, :] + jnp.dot(
                p.astype(BF16), vx, preferred_element_type=F32)
            m_scr[rows, :] = m_new

    scores(0, 0)

    def pair(jj, carry):
        j = 2 * jj
        scores(j + 1, 1)
        accumulate(j, 0, False)
        scores(j + 2, 0)
        accumulate(j + 1, 1, False)
        return carry

    lax.fori_loop(0, i // 2, pair, 0)

    @pl.when(i % 2 == 1)
    def _odd_tail():
        scores(i, 1)
        accumulate(i - 1, 0, False)
        accumulate(i, 1, True)

    @pl.when(i % 2 == 0)
    def _even_tail():
        accumulate(i, 0, True)

    lam = (jnp.exp(jnp.sum(lq1_ref[...] * lk1_ref[...], axis=-1, keepdims=True))
           - jnp.exp(jnp.sum(lq2_ref[...] * lk2_ref[...], axis=-1, keepdims=True))
           + lambda_init)
    o0 = acc_scr[0:tq, 0:B_V_DIM] / acc_scr[0:tq, B_V_DIM:B_V_DIM + 1]
    o1 = acc_scr[tq:2 * tq, 0:B_V_DIM] / acc_scr[tq:2 * tq, B_V_DIM:B_V_DIM + 1]
    o = o0 - lam * o1
    o = o * lax.rsqrt(jnp.mean(o * o, axis=-1, keepdims=True) + LN_EPS)
    o = o * g_ref[...] * (1.0 - lambda_init)
    o_ref[...] = o.astype(o_ref.dtype)


def _diff_attention(qkv, lq1, lk1, lq2, lk2, subln_g, lambda_init, tq=512):
    s = qkv.shape[0]
    tq = _tile(s, tq)
    assert tq <= 2 * _POS_LOW and tq % _POS_LOW == 0, tq
    vec = lambda n: pl.BlockSpec((1, n), lambda h, i: (0, 0))
    kern = functools.partial(_diff_kernel, tq=tq, lambda_init=lambda_init)
    return pl.pallas_call(
        kern,
        grid=(B_HEADS, s // tq),
        in_specs=[pl.BlockSpec((tq, B_V_DIM), lambda h, i: (i, h)),
                  pl.BlockSpec((s, B_V_DIM), lambda h, i: (0, B_K_BLOCK0 + h)),
                  pl.BlockSpec((s, B_V_DIM), lambda h, i: (0, B_V_BLOCK0 + h)),
                  vec(B_HEAD_DIM), vec(B_HEAD_DIM), vec(B_HEAD_DIM), vec(B_HEAD_DIM),
                  vec(B_V_DIM), pl.BlockSpec(memory_space=pltpu.SMEM)],
        out_specs=pl.BlockSpec((tq, B_V_DIM), lambda h, i: (i, h)),
        out_shape=jax.ShapeDtypeStruct((s, B_HEADS * B_V_DIM), BF16),
        scratch_shapes=[pltpu.VMEM((2 * tq, 2 * LANES), BF16), pltpu.VMEM((tq, LANES), BF16),
                        pltpu.VMEM((tq, LANES), BF16), pltpu.VMEM((2 * tq, 1), F32),
                        pltpu.VMEM((2 * tq, 2 * LANES), F32), pltpu.VMEM((2, 2 * tq, tq), F32)],
        compiler_params=_params("arbitrary", "arbitrary"),
        name="diff_attention",
    )(qkv, qkv, qkv, lq1.reshape(1, -1).astype(F32), lk1.reshape(1, -1).astype(F32),
      lq2.reshape(1, -1).astype(F32), lk2.reshape(1, -1).astype(F32),
      subln_g.reshape(1, -1).astype(F32), _alibi_slopes(B_HEADS))


_N_RANKED = P_TOPK + 1
_PAIR_RANKS = [(i, j) for i in range(_N_RANKED) for j in range(_N_RANKED)
               if (i + 1) * (j + 1) <= _N_RANKED]
_CAND_ROWS = -(-len(_PAIR_RANKS) // 8) * 8


def _top_values(work, count):
    out = []
    for _ in range(count):
        mx = jnp.max(work, axis=0, keepdims=True)
        out.append(mx)
        work = jnp.where(work == mx, -jnp.inf, work)
    return out


def _peer_score_kernel(q_ref, sk_ref, thr_ref, e1_ref, a2_ref, e2_ref, cand_scr):
    tt = q_ref.shape[0]
    cand_scr[...] = jnp.full(cand_scr.shape, -jnp.inf, F32)
    for h in range(P_HEADS):
        st = []
        for c in range(2):
            qs = q_ref[:, (2 * h + c) * P_HALF:(2 * h + c + 1) * P_HALF]
            st.append(lax.dot_general(sk_ref[h, c], qs, (((1,), (1,)), ((), ())),
                                      preferred_element_type=F32))
        a = _top_values(st[0], _N_RANKED)
        b = _top_values(st[1], _N_RANKED)
        for r, (i, j) in enumerate(_PAIR_RANKS):
            cand_scr[r:r + 1, :] = a[i] + b[j]
        best = _top_values(cand_scr[...], _N_RANKED)
        z = jnp.zeros((1, tt), F32)
        for v in best[:P_TOPK]:
            z = z + jnp.exp(v - best[0])
        cut = 0.5 * (best[P_TOPK - 1] + best[P_TOPK])
        thr_ref[h] = cut - st[0]
        a2_ref[h] = st[1]
        e1_ref[h] = jnp.exp(st[0] - a[0]) / z
        e2_ref[h] = jnp.exp(st[1] - b[0])


def _peer_scores(q, subkeys, tt=512):
    s = q.shape[0]
    tt = _tile(s, tt)
    big = jax.ShapeDtypeStruct((P_HEADS, N_KEYS, s), F32)
    big_spec = pl.BlockSpec((P_HEADS, N_KEYS, tt), lambda t: (0, 0, t))
    return pl.pallas_call(
        _peer_score_kernel,
        grid=(s // tt,),
        in_specs=[pl.BlockSpec((tt, q.shape[1]), lambda t: (t, 0)),
                  pl.BlockSpec(subkeys.shape, lambda t: (0, 0, 0, 0))],
        out_specs=[big_spec, big_spec, big_spec, big_spec],
        out_shape=[big, big, big, big],
        scratch_shapes=[pltpu.VMEM((_CAND_ROWS, tt), F32)],
        compiler_params=_params("parallel"),
        name="peer_scores",
    )(q, subkeys)


def _gelu(a):
    return 0.5 * a * (1.0 + lax.erf(a * np.float32(np.sqrt(0.5))))


def _peer_dense_kernel(x_ref, u_ref, vt_ref, thr_ref, e1_ref, a2_ref, e2_ref,
                       o_ref, act_scr, h_scr):
    e = pl.program_id(1)
    te, tt = act_scr.shape

    @pl.when(e == 0)
    def _init():
        o_ref[...] = jnp.zeros(o_ref.shape, F32)

    act_scr[...] = lax.dot_general(u_ref[...], x_ref[...], (((1,), (1,)), ((), ())),
                                   preferred_element_type=F32)
    for il in range(te // N_KEYS):
        rows = slice(il * N_KEYS, (il + 1) * N_KEYS)
        for lg in range(tt // LANES):
            cols = slice(lg * LANES, (lg + 1) * LANES)
            gate = jnp.zeros((N_KEYS, LANES), F32)
            for h in range(P_HEADS):
                w = e2_ref[h, :, cols] * e1_ref[h, il:il + 1, cols]
                gate = gate + jnp.where(a2_ref[h, :, cols] > thr_ref[h, il:il + 1, cols], w, 0.0)
            h_scr[rows, cols] = (gate * _gelu(act_scr[rows, cols])).astype(BF16)
    o_ref[...] += jnp.dot(vt_ref[...], h_scr[...], preferred_element_type=F32)


def _peer_dense(x_bf, u_bf, vt_bf, thr, e1, a2, e2, tt=512, te=1024):
    s, d = x_bf.shape
    n_exp = u_bf.shape[0]
    tt, te = _tile(s, tt), _tile(n_exp, te)
    rows_spec = pl.BlockSpec((P_HEADS, te // N_KEYS, tt), lambda t, e: (0, e, t))
    full_spec = pl.BlockSpec((P_HEADS, N_KEYS, tt), lambda t, e: (0, 0, t))
    return pl.pallas_call(
        _peer_dense_kernel,
        grid=(s // tt, n_exp // te),
        in_specs=[pl.BlockSpec((tt, d), lambda t, e: (t, 0)),
                  pl.BlockSpec((te, d), lambda t, e: (e, 0)),
                  pl.BlockSpec((d, te), lambda t, e: (0, e)),
                  rows_spec, rows_spec, full_spec, full_spec],
        out_specs=pl.BlockSpec((d, tt), lambda t, e: (0, t)),
        out_shape=jax.ShapeDtypeStruct((d, s), F32),
        scratch_shapes=[pltpu.VMEM((te, tt), F32), pltpu.VMEM((te, tt), BF16)],
        compiler_params=_params("parallel", "arbitrary"),
        name="peer_dense",
    )(x_bf, u_bf, vt_bf, thr, e1, a2, e2)


def _peer_ffn(x_bf, w_q, subkeys, u, v):
    q = _matmul(x_bf, w_q.astype(BF16), BF16)
    thr, e1, a2, e2 = _peer_scores(q, subkeys.astype(BF16))
    return _peer_dense(x_bf, u.astype(BF16), v.astype(BF16).T, thr, e1, a2, e2).T


def kernel(x, a_w_qkv, a_sinks, a_w_o, b_w_qkv, b_lambda_q1, b_lambda_k1, b_lambda_q2,
           b_lambda_k2, b_subln_g, b_w_o, ln1_g, ln1_b, ln2_g, ln2_b,
           peer_w_q, peer_subkeys, peer_u, peer_v):
    bsz, seq, d = x.shape
    xf = x.reshape(bsz * seq, d).astype(F32)
    assert bsz == 1, "attention kernels index one sequence"
    xb = xf.astype(BF16)
    for i in range(DEPTH):
        j = i // 2
        if i % 2 == 0:
            qkv = _matmul(xb, _swa_weight(a_w_qkv[j]), BF16)
            mix = _swa_attention(qkv, a_sinks[j])
            w_o = a_w_o[j]
        else:
            lambda_init = 0.8 - 0.6 * float(np.exp(-0.3 * i))
            qkv = _matmul(xb, b_w_qkv[j].astype(BF16), BF16)
            mix = _diff_attention(qkv, b_lambda_q1[j], b_lambda_k1[j], b_lambda_q2[j],
                                  b_lambda_k2[j], b_subln_g[j], lambda_init)
            w_o = b_w_o[j]
        xf, xb = _proj_residual_ln(mix, w_o.astype(BF16), xf, ln1_g[i], ln1_b[i])
        ffn = _peer_ffn(xb, peer_w_q[i], peer_subkeys[i], peer_u[i], peer_v[i])
        xf, xb = _residual_ln(ffn, xf, ln2_g[i], ln2_b[i])
    return xf.reshape(bsz, seq, d).astype(x.dtype)
```

```python
import functools

import jax
import jax.numpy as jnp
import numpy as np
from jax import lax
from jax.experimental import pallas as pl
from jax.experimental.pallas import tpu as pltpu

F32 = jnp.float32
BF16 = jnp.bfloat16

DEPTH = 4
A_HEADS, A_KV_HEADS, A_HEAD_DIM, WINDOW = 32, 4, 64, 128
A_GROUP = A_HEADS // A_KV_HEADS
B_HEADS, B_HEAD_DIM = 16, 64
B_V_DIM = 2 * B_HEAD_DIM
P_HEADS, N_KEYS, P_TOPK, P_HALF = 8, 128, 16, 128
LN_EPS = 1e-5
NEG = -1e30
ALPHA = (2.0 * DEPTH) ** 0.25

LANES = 128
V7X_VMEM_LIMIT_BYTES = 56 * 1024 * 1024


def _tile(n, pref):
    t = min(n, pref)
    assert n % t == 0, (n, t)
    return t


def _params(*sem, flags=None):
    return pltpu.CompilerParams(dimension_semantics=sem, vmem_limit_bytes=V7X_VMEM_LIMIT_BYTES,
                                flags=flags)


def _alibi_slopes(n):
    return jnp.asarray((2.0 ** (-8.0 * np.arange(1, n + 1) / n)).astype(np.float32))


def _mm_kernel(a_ref, b_ref, o_ref):
    o_ref[...] = jnp.dot(a_ref[...], b_ref[...], preferred_element_type=F32).astype(o_ref.dtype)


def _matmul(a, b, out_dtype, tm=1024, tn=512):
    m, k = a.shape
    n = b.shape[1]
    tm, tn = _tile(m, tm), _tile(n, tn)
    return pl.pallas_call(
        _mm_kernel,
        grid=(m // tm, n // tn),
        in_specs=[pl.BlockSpec((tm, k), lambda i, j: (i, 0)),
                  pl.BlockSpec((k, tn), lambda i, j: (0, j))],
        out_specs=pl.BlockSpec((tm, tn), lambda i, j: (i, j)),
        out_shape=jax.ShapeDtypeStruct((m, n), out_dtype),
        compiler_params=_params("parallel", "parallel"),
        name="matmul",
    )(a, b)


def _layer_norm_rows(z, g, b):
    mu = jnp.mean(z, axis=-1, keepdims=True)
    zc = z - mu
    var = jnp.mean(zc * zc, axis=-1, keepdims=True)
    return zc * lax.rsqrt(var + LN_EPS) * g + b


def _proj_ln_kernel(a_ref, w_ref, x_ref, g_ref, b_ref, of_ref, ob_ref):
    y = jnp.dot(a_ref[...], w_ref[...], preferred_element_type=F32)
    out = _layer_norm_rows(ALPHA * x_ref[...] + y, g_ref[...], b_ref[...])
    of_ref[...] = out
    ob_ref[...] = out.astype(BF16)


def _proj_residual_ln(a, w, x, g, b, tm=256):
    s, k = a.shape
    d = w.shape[1]
    tm = _tile(s, tm)
    row = lambda i: (i, 0)
    fixed = lambda i: (0, 0)
    return pl.pallas_call(
        _proj_ln_kernel,
        grid=(s // tm,),
        in_specs=[pl.BlockSpec((tm, k), row), pl.BlockSpec((k, d), fixed),
                  pl.BlockSpec((tm, d), row), pl.BlockSpec((1, d), fixed),
                  pl.BlockSpec((1, d), fixed)],
        out_specs=[pl.BlockSpec((tm, d), row), pl.BlockSpec((tm, d), row)],
        out_shape=[jax.ShapeDtypeStruct((s, d), F32), jax.ShapeDtypeStruct((s, d), BF16)],
        compiler_params=_params("parallel"),
        name="proj_residual_ln",
    )(a, w, x, g.reshape(1, d), b.reshape(1, d))


A_Q_COLS = A_HEADS * A_HEAD_DIM
A_PAIR_COLS = 2 * A_HEAD_DIM
A_K_BLOCK0 = A_Q_COLS // A_PAIR_COLS
A_V_BLOCK0 = A_K_BLOCK0 + A_KV_HEADS
A_GROUP_COLS = A_GROUP * A_HEAD_DIM


def _swa_weight(w_qkv):
    d = w_qkv.shape[0]
    wq = w_qkv[:, :A_Q_COLS]
    wk = w_qkv[:, A_Q_COLS:A_Q_COLS + A_KV_HEADS * A_HEAD_DIM].reshape(d, A_KV_HEADS, A_HEAD_DIM)
    wv = w_qkv[:, A_Q_COLS + A_KV_HEADS * A_HEAD_DIM:].reshape(d, A_KV_HEADS, A_HEAD_DIM)
    dup = lambda w: jnp.concatenate([w, w], axis=-1).reshape(d, A_KV_HEADS * A_PAIR_COLS)
    return jnp.concatenate([wq, dup(wk), dup(wv)], axis=1).astype(BF16)


def _half_masks(dtype):
    lane = lax.broadcasted_iota(jnp.int32, (1, LANES), 1)
    lo = (lane < LANES // 2).astype(F32)
    return lo.astype(dtype), (1.0 - lo).astype(dtype)


def _swa_kernel(q_ref, kp_ref, kc_ref, vp_ref, vc_ref, sink_ref, slope_ref, o_ref):
    i = pl.program_id(0)
    g = pl.program_id(1)
    blk = WINDOW
    kband = jnp.concatenate([kp_ref[...], kc_ref[...]], axis=0)
    vband = jnp.concatenate([vp_ref[...], vc_ref[...]], axis=0)
    row = lax.broadcasted_iota(jnp.int32, (2 * blk, 2 * blk), 0)
    col = lax.broadcasted_iota(jnp.int32, (2 * blk, 2 * blk), 1)
    dist = jnp.bitwise_and(row, blk - 1) + blk - col
    valid = (dist >= 0) & (dist < WINDOW) & ((col >= blk) | (i > 0))
    distf = dist.astype(F32)
    top = row < blk
    top_col = top[:, :1]
    mlo, mhi = _half_masks(BF16)
    lo_f = lax.broadcasted_iota(jnp.int32, (blk, LANES), 1) < LANES // 2
    for p in range(A_GROUP // 2):
        h0 = g * A_GROUP + 2 * p
        q2 = q_ref[:, p * LANES:(p + 1) * LANES]
        lhs = jnp.concatenate([q2 * mlo, q2 * mhi], axis=0)
        s = lax.dot_general(lhs, kband, (((1,), (1,)), ((), ())),
                            preferred_element_type=F32) * (A_HEAD_DIM ** -0.5)
        slope = jnp.where(top, slope_ref[h0], slope_ref[h0 + 1])
        sink = jnp.where(top_col, sink_ref[h0], sink_ref[h0 + 1])
        s = s - slope * distf
        s = jnp.where(valid, s, NEG)
        m = jnp.maximum(jnp.max(s, axis=-1, keepdims=True), sink)
        pexp = jnp.exp(s - m)
        denom = jnp.sum(pexp, axis=-1, keepdims=True) + jnp.exp(sink - m)
        probs = (pexp / denom).astype(BF16)
        o2 = jnp.dot(probs, vband, preferred_element_type=F32)
        o_ref[:, p * LANES:(p + 1) * LANES] = jnp.where(lo_f, o2[:blk], o2[blk:]).astype(o_ref.dtype)


def _swa_attention(qkv, sinks):
    s = qkv.shape[0]
    nb = s // WINDOW
    prev = lambda i: jnp.maximum(i - 1, 0)
    smem = pl.BlockSpec(memory_space=pltpu.SMEM)
    return pl.pallas_call(
        _swa_kernel,
        grid=(nb, A_KV_HEADS),
        in_specs=[pl.BlockSpec((WINDOW, A_GROUP_COLS), lambda i, g: (i, g)),
                  pl.BlockSpec((WINDOW, A_PAIR_COLS), lambda i, g: (prev(i), A_K_BLOCK0 + g)),
                  pl.BlockSpec((WINDOW, A_PAIR_COLS), lambda i, g: (i, A_K_BLOCK0 + g)),
                  pl.BlockSpec((WINDOW, A_PAIR_COLS), lambda i, g: (prev(i), A_V_BLOCK0 + g)),
                  pl.BlockSpec((WINDOW, A_PAIR_COLS), lambda i, g: (i, A_V_BLOCK0 + g)),
                  smem, smem],
        out_specs=pl.BlockSpec((WINDOW, A_GROUP_COLS), lambda i, g: (i, g)),
        out_shape=jax.ShapeDtypeStruct((s, A_Q_COLS), BF16),
        compiler_params=_params("parallel", "parallel"),
        name="swa_attention",
    )(qkv, qkv, qkv, qkv, qkv, sinks.astype(F32), _alibi_slopes(A_HEADS))


B_QK_COLS = B_HEADS * 2 * B_HEAD_DIM
B_K_BLOCK0 = B_QK_COLS // B_V_DIM
B_V_BLOCK0 = 2 * B_K_BLOCK0


_ALIBI_COLS = 12
_POS_LOW = 256
_EXP_ZERO_BELOW = 110.0
_BOUND_SLACK = 1.001
_BOUND_MARGIN = 1.0


def _diff_head_tables(slope, lhs_scr, kx_scr, vx_scr, tq):
    sv = jnp.full((1, LANES), slope, F32)
    s1 = sv.astype(BF16).astype(F32)
    s2 = (sv - s1).astype(BF16).astype(F32)
    s3 = (sv - s1 - s2).astype(BF16).astype(F32)
    lane = lax.broadcasted_iota(jnp.int32, (1, LANES), 1)
    third = lambda k: (lane == k) | (lane == k + 3) | (lane == k + 6) | (lane == k + 9)
    piece = jnp.where(third(0), s1, jnp.where(third(1), s2, s3))

    row = lax.broadcasted_iota(jnp.int32, (2 * tq, LANES), 0)
    lane_q = lax.broadcasted_iota(jnp.int32, (2 * tq, LANES), 1)
    r = jnp.where(row >= tq, row - tq, row)
    r_lo = jnp.bitwise_and(r, _POS_LOW - 1).astype(F32)
    r_hi = jnp.where(r >= _POS_LOW, 1.0, 0.0)
    ext = jnp.where(lane_q < 3, r_lo, jnp.where(lane_q < 6, r_hi,
                    jnp.where(lane_q < _ALIBI_COLS, piece, 0.0)))
    lhs_scr[:, LANES:2 * LANES] = ext.astype(BF16)

    c = lax.broadcasted_iota(jnp.int32, (tq, LANES), 0)
    lane_k = lax.broadcasted_iota(jnp.int32, (tq, LANES), 1)
    c_lo = jnp.bitwise_and(c, _POS_LOW - 1).astype(F32)
    c_hi = jnp.where(c >= _POS_LOW, float(_POS_LOW), 0.0)
    kext = jnp.where(lane_k < 3, -piece, jnp.where(lane_k < 6, -float(_POS_LOW) * piece,
                     jnp.where(lane_k < 9, c_lo, jnp.where(lane_k < _ALIBI_COLS, c_hi, 0.0))))
    kx_scr[...] = kext.astype(BF16)
    vx_scr[...] = jnp.where(lane_k == 0, 1.0, 0.0).astype(BF16)


def _diff_kernel(q_ref, k_ref, v_ref, lq1_ref, lk1_ref, lq2_ref, lk2_ref, g_ref, slope_ref,
                 o_ref, lhs_scr, kx_scr, vx_scr, m_scr, acc_scr, s_scr, k2max_scr, *, tq,
                 lambda_init):
    h = pl.program_id(0)
    i = pl.program_id(1)
    slope = slope_ref[h]

    @pl.when(i == 0)
    def _head_init():
        _diff_head_tables(slope, lhs_scr, kx_scr, vx_scr, tq)
        k2 = jnp.square(k_ref[...].astype(F32))
        k2max = jnp.maximum(*[jnp.max(jnp.sum(k2 * msk, axis=-1, keepdims=True), axis=0,
                                      keepdims=True) for msk in _half_masks(F32)])
        k2max_scr[...] = jnp.broadcast_to(k2max, k2max_scr.shape)

    mlo, mhi = _half_masks(BF16)
    q = q_ref[...] * (B_HEAD_DIM ** -0.5)
    lhs_scr[0:tq, 0:LANES] = q * mlo
    lhs_scr[tq:2 * tq, 0:LANES] = q * mhi
    m_scr[...] = jnp.full(m_scr.shape, -jnp.inf, F32)
    acc_scr[...] = jnp.zeros(acc_scr.shape, F32)

    def scores(j, slot):
        start = pl.multiple_of(j * tq, tq)
        kx = jnp.concatenate([k_ref[pl.ds(start, tq), :], kx_scr[...]], axis=1)
        s_scr[slot] = lax.dot_general(lhs_scr[...], kx, (((1,), (1,)), ((), ())),
                                      preferred_element_type=F32)

    def accumulate(j, slot, diagonal):
        start = pl.multiple_of(j * tq, tq)
        vx = jnp.concatenate([v_ref[pl.ds(start, tq), :], vx_scr[...]], axis=1)
        for half in range(2):
            rows = slice(half * tq, (half + 1) * tq)
            s = s_scr[slot, rows, :]
            if diagonal:
                row = lax.broadcasted_iota(jnp.int32, (tq, tq), 0)
                col = lax.broadcasted_iota(jnp.int32, (tq, tq), 1)
                s = jnp.where(row >= col, s, NEG)
                off = 0.0
            else:
                off = slope * ((i - j) * tq).astype(F32)
            m_old = m_scr[rows, :]
            m_new = jnp.maximum(m_old, jnp.max(s, axis=-1, keepdims=True) - off)
            p = jnp.exp(s - (m_new + off))
            alpha = jnp.exp(m_old - m_new)
            acc_scr[rows, :] = alpha * acc_scr[rows, :] + jnp.dot(
                p.astype(BF16), vx, preferred_element_type=F32)
            m_scr[rows, :] = m_new

    scores(i, 0)
    accumulate(i, 0, True)
    qf = (q * mlo).astype(F32), (q * mhi).astype(F32)
    q2max = jnp.maximum(*[jnp.max(jnp.sum(c * c, axis=-1, keepdims=True), axis=0, keepdims=True)
                          for c in qf])
    m_min = jnp.min(m_scr[...], axis=0, keepdims=True)
    reach = (jnp.sqrt(q2max * k2max_scr[0:1, 0:1]) * _BOUND_SLACK + slope * (tq - 1)
             - m_min + _EXP_ZERO_BELOW + _BOUND_MARGIN) / (slope * tq)
    n_far = jnp.minimum(reach, i.astype(F32)).astype(jnp.int32)[0, 0]

    @pl.when(n_far > 0)
    def _prefetch():
        scores(i - 1, 0)

    def pair(jj, carry):
        j = i - 1 - 2 * jj
        scores(j - 1, 1)
        accumulate(j, 0, False)
        scores(jnp.maximum(j - 2, 0), 0)
        accumulate(j - 1, 1, False)
        return carry

    lax.fori_loop(0, n_far // 2, pair, 0)

    @pl.when(n_far % 2 == 1)
    def _odd_tail():
        accumulate(i - n_far, 0, False)

    lam = (jnp.exp(jnp.sum(lq1_ref[...] * lk1_ref[...], axis=-1, keepdims=True))
           - jnp.exp(jnp.sum(lq2_ref[...] * lk2_ref[...], axis=-1, keepdims=True))
           + lambda_init)
    o0 = acc_scr[0:tq, 0:B_V_DIM] / acc_scr[0:tq, B_V_DIM:B_V_DIM + 1]
    o1 = acc_scr[tq:2 * tq, 0:B_V_DIM] / acc_scr[tq:2 * tq, B_V_DIM:B_V_DIM + 1]
    o = o0 - lam * o1
    o = o * lax.rsqrt(jnp.mean(o * o, axis=-1, keepdims=True) + LN_EPS)
    o = o * g_ref[...] * (1.0 - lambda_init)
    o_ref[...] = o.astype(o_ref.dtype)


def _diff_attention(qkv, lq1, lk1, lq2, lk2, subln_g, lambda_init, tq=512):
    s = qkv.shape[0]
    tq = _tile(s, tq)
    assert tq <= 2 * _POS_LOW and tq % _POS_LOW == 0, tq
    vec = lambda n: pl.BlockSpec((1, n), lambda h, i: (0, 0))
    kern = functools.partial(_diff_kernel, tq=tq, lambda_init=lambda_init)
    return pl.pallas_call(
        kern,
        grid=(B_HEADS, s // tq),
        in_specs=[pl.BlockSpec((tq, B_V_DIM), lambda h, i: (i, h)),
                  pl.BlockSpec((s, B_V_DIM), lambda h, i: (0, B_K_BLOCK0 + h)),
                  pl.BlockSpec((s, B_V_DIM), lambda h, i: (0, B_V_BLOCK0 + h)),
                  vec(B_HEAD_DIM), vec(B_HEAD_DIM), vec(B_HEAD_DIM), vec(B_HEAD_DIM),
                  vec(B_V_DIM), pl.BlockSpec(memory_space=pltpu.SMEM)],
        out_specs=pl.BlockSpec((tq, B_V_DIM), lambda h, i: (i, h)),
        out_shape=jax.ShapeDtypeStruct((s, B_HEADS * B_V_DIM), BF16),
        scratch_shapes=[pltpu.VMEM((2 * tq, 2 * LANES), BF16), pltpu.VMEM((tq, LANES), BF16),
                        pltpu.VMEM((tq, LANES), BF16), pltpu.VMEM((2 * tq, 1), F32),
                        pltpu.VMEM((2 * tq, 2 * LANES), F32), pltpu.VMEM((2, 2 * tq, tq), F32),
                        pltpu.VMEM((8, LANES), F32)],
        compiler_params=_params("arbitrary", "arbitrary"),
        name="diff_attention",
    )(qkv, qkv, qkv, lq1.reshape(1, -1).astype(F32), lk1.reshape(1, -1).astype(F32),
      lq2.reshape(1, -1).astype(F32), lk2.reshape(1, -1).astype(F32),
      subln_g.reshape(1, -1).astype(F32), _alibi_slopes(B_HEADS))


_N_RANKED = P_TOPK + 1
_PAIR_RANKS = [(i, j) for i in range(_N_RANKED) for j in range(_N_RANKED)
               if (i + 1) * (j + 1) <= _N_RANKED]
_CAND_ROWS = -(-len(_PAIR_RANKS) // 8) * 8


def _top_values(work, count):
    out = []
    for _ in range(count):
        mx = jnp.max(work, axis=0, keepdims=True)
        out.append(mx)
        work = jnp.where(work == mx, -jnp.inf, work)
    return out


def _peer_score_kernel(q_ref, sk_ref, thr_ref, e1_ref, a2_ref, e2_ref, cand_scr):
    tt = q_ref.shape[0]
    cand_scr[...] = jnp.full(cand_scr.shape, -jnp.inf, F32)
    for h in range(P_HEADS):
        st = []
        for c in range(2):
            qs = q_ref[:, (2 * h + c) * P_HALF:(2 * h + c + 1) * P_HALF]
            st.append(lax.dot_general(sk_ref[h, c], qs, (((1,), (1,)), ((), ())),
                                      preferred_element_type=F32))
        a = _top_values(st[0], _N_RANKED)
        b = _top_values(st[1], _N_RANKED)
        for r, (i, j) in enumerate(_PAIR_RANKS):
            cand_scr[r:r + 1, :] = a[i] + b[j]
        best = _top_values(cand_scr[...], _N_RANKED)
        z = jnp.zeros((1, tt), F32)
        for v in best[:P_TOPK]:
            z = z + jnp.exp(v - best[0])
        cut = 0.5 * (best[P_TOPK - 1] + best[P_TOPK])
        thr_ref[h] = cut - st[0]
        a2_ref[h] = st[1]
        e1_ref[h] = jnp.exp(st[0] - a[0]) / z
        e2_ref[h] = jnp.exp(st[1] - b[0])


def _peer_scores(q, subkeys, tt=512):
    s = q.shape[0]
    tt = _tile(s, tt)
    big = jax.ShapeDtypeStruct((P_HEADS, N_KEYS, s), F32)
    big_spec = pl.BlockSpec((P_HEADS, N_KEYS, tt), lambda t: (0, 0, t))
    return pl.pallas_call(
        _peer_score_kernel,
        grid=(s // tt,),
        in_specs=[pl.BlockSpec((tt, q.shape[1]), lambda t: (t, 0)),
                  pl.BlockSpec(subkeys.shape, lambda t: (0, 0, 0, 0))],
        out_specs=[big_spec, big_spec, big_spec, big_spec],
        out_shape=[big, big, big, big],
        scratch_shapes=[pltpu.VMEM((_CAND_ROWS, tt), F32)],
        compiler_params=_params("parallel"),
        name="peer_scores",
    )(q, subkeys)


def _gelu(a):
    return 0.5 * a * (1.0 + lax.erf(a * np.float32(np.sqrt(0.5))))


def _peer_dense_kernel(x_ref, u_ref, v_ref, thr_ref, e1_ref, a2_ref, e2_ref, xres_ref, g_ref,
                       b_ref, of_ref, ob_ref, act_scr, h_scr, acc_scr, *, n_etiles):
    s = pl.program_id(0)
    _, te, tt = act_scr.shape
    e_prev = lax.rem(jnp.maximum(s - 1, 0), n_etiles)

    @pl.when(s == 0)
    def _first():
        act_scr[1] = jnp.zeros((te, tt), F32)

    @pl.when(e_prev == 0)
    def _init():
        acc_scr[...] = jnp.zeros(acc_scr.shape, F32)

    def body(write_slot, read_slot):
        act_scr[write_slot] = lax.dot_general(u_ref[...], x_ref[...], (((1,), (1,)), ((), ())),
                                              preferred_element_type=F32)
        for il in range(te // N_KEYS):
            rows = slice(il * N_KEYS, (il + 1) * N_KEYS)
            for lg in range(tt // LANES):
                cols = slice(lg * LANES, (lg + 1) * LANES)
                gate = jnp.zeros((N_KEYS, LANES), F32)
                for h in range(P_HEADS):
                    w = e2_ref[h, :, cols] * e1_ref[h, il:il + 1, cols]
                    gate = jnp.where(a2_ref[h, :, cols] > thr_ref[h, il:il + 1, cols],
                                     gate + w, gate)
                h_scr[rows, cols] = (gate * _gelu(act_scr[read_slot, rows, cols])).astype(BF16)
        acc_scr[...] += lax.dot_general(h_scr[...], v_ref[...], (((0,), (0,)), ((), ())),
                                        preferred_element_type=F32)

    @pl.when(lax.rem(s, 2) == 0)
    def _even():
        body(0, 1)

    @pl.when(lax.rem(s, 2) == 1)
    def _odd():
        body(1, 0)

    @pl.when((e_prev == n_etiles - 1) & (s > 0))
    def _finish():
        out = _layer_norm_rows(ALPHA * xres_ref[...] + acc_scr[...], g_ref[...], b_ref[...])
        of_ref[...] = out
        ob_ref[...] = out.astype(BF16)


def _peer_dense_ln(x_bf, u_bf, v_bf, thr, e1, a2, e2, x_res, g, b, tt=512, te=1024):
    s, d = x_bf.shape
    n_exp = u_bf.shape[0]
    tt, te = _tile(s, tt), _tile(n_exp, te)
    ne = n_exp // te
    n_items = (s // tt) * ne
    cur = lambda i: jnp.minimum(i, n_items - 1)
    prev = lambda i: jnp.maximum(i - 1, 0)
    rows_spec = pl.BlockSpec((P_HEADS, te // N_KEYS, tt),
                             lambda i: (0, prev(i) % ne, prev(i) // ne))
    once = pl.Buffered(1)
    full_spec = pl.BlockSpec((P_HEADS, N_KEYS, tt), lambda i: (0, 0, prev(i) // ne),
                             pipeline_mode=once)
    tok_spec = pl.BlockSpec((tt, d), lambda i: (prev(i) // ne, 0))
    res_spec = pl.BlockSpec((tt, d), lambda i: (prev(i) // ne, 0), pipeline_mode=once)
    vec_spec = pl.BlockSpec((1, d), lambda i: (0, 0))
    return pl.pallas_call(
        functools.partial(_peer_dense_kernel, n_etiles=ne),
        grid=(n_items + 1,),
        in_specs=[pl.BlockSpec((tt, d), lambda i: (cur(i) // ne, 0)),
                  pl.BlockSpec((te, d), lambda i: (cur(i) % ne, 0)),
                  pl.BlockSpec((te, d), lambda i: (prev(i) % ne, 0)),
                  rows_spec, rows_spec, full_spec, full_spec, res_spec, vec_spec, vec_spec],
        out_specs=[tok_spec, tok_spec],
        out_shape=[jax.ShapeDtypeStruct((s, d), F32), jax.ShapeDtypeStruct((s, d), BF16)],
        scratch_shapes=[pltpu.VMEM((2, te, tt), F32), pltpu.VMEM((te, tt), BF16),
                        pltpu.VMEM((tt, d), F32)],
        compiler_params=_params("arbitrary"),
        name="peer_dense",
    )(x_bf, u_bf, v_bf, thr, e1, a2, e2, x_res, g.reshape(1, d), b.reshape(1, d))


def _peer_ffn_ln(x_f, x_bf, w_q, subkeys, u, v, g, b):
    q = _matmul(x_bf, w_q.astype(BF16), BF16)
    thr, e1, a2, e2 = _peer_scores(q, subkeys.astype(BF16))
    return _peer_dense_ln(x_bf, u.astype(BF16), v.astype(BF16), thr, e1, a2, e2, x_f, g, b)


def kernel(x, a_w_qkv, a_sinks, a_w_o, b_w_qkv, b_lambda_q1, b_lambda_k1, b_lambda_q2,
           b_lambda_k2, b_subln_g, b_w_o, ln1_g, ln1_b, ln2_g, ln2_b,
           peer_w_q, peer_subkeys, peer_u, peer_v):
    bsz, seq, d = x.shape
    xf = x.reshape(bsz * seq, d).astype(F32)
    assert bsz == 1, "attention kernels index one sequence"
    xb = xf.astype(BF16)
    for i in range(DEPTH):
        j = i // 2
        if i % 2 == 0:
            qkv = _matmul(xb, _swa_weight(a_w_qkv[j]), BF16)
            mix = _swa_attention(qkv, a_sinks[j])
            w_o = a_w_o[j]
        else:
            lambda_init = 0.8 - 0.6 * float(np.exp(-0.3 * i))
            qkv = _matmul(xb, b_w_qkv[j].astype(BF16), BF16)
            mix = _diff_attention(qkv, b_lambda_q1[j], b_lambda_k1[j], b_lambda_q2[j],
                                  b_lambda_k2[j], b_subln_g[j], lambda_init)
            w_o = b_w_o[j]
        xf, xb = _proj_residual_ln(mix, w_o.astype(BF16), xf, ln1_g[i], ln1_b[i])
        xf, xb = _peer_ffn_ln(xf, xb, peer_w_q[i], peer_subkeys[i], peer_u[i], peer_v[i],
                              ln2_g[i], ln2_b[i])
    return xf.reshape(bsz, seq, d).astype(x.dtype)
```

```python
import functools

import jax
import jax.numpy as jnp
import numpy as np
from jax import lax
from jax.experimental import pallas as pl
from jax.experimental.pallas import tpu as pltpu

F32 = jnp.float32
BF16 = jnp.bfloat16

DEPTH = 4
A_HEADS, A_KV_HEADS, A_HEAD_DIM, WINDOW = 32, 4, 64, 128
A_GROUP = A_HEADS // A_KV_HEADS
B_HEADS, B_HEAD_DIM = 16, 64
B_V_DIM = 2 * B_HEAD_DIM
P_HEADS, N_KEYS, P_TOPK, P_HALF = 8, 128, 16, 128
LN_EPS = 1e-5
NEG = -1e30
ALPHA = (2.0 * DEPTH) ** 0.25

LANES = 128
V7X_VMEM_LIMIT_BYTES = 56 * 1024 * 1024


def _tile(n, pref):
    t = min(n, pref)
    assert n % t == 0, (n, t)
    return t


def _params(*sem, flags=None):
    return pltpu.CompilerParams(dimension_semantics=sem, vmem_limit_bytes=V7X_VMEM_LIMIT_BYTES,
                                flags=flags)


def _alibi_slopes(n):
    return jnp.asarray((2.0 ** (-8.0 * np.arange(1, n + 1) / n)).astype(np.float32))


def _mm_kernel(a_ref, b_ref, o_ref):
    o_ref[...] = jnp.dot(a_ref[...], b_ref[...], preferred_element_type=F32).astype(o_ref.dtype)


def _matmul(a, b, out_dtype, tm=1024, tn=512):
    m, k = a.shape
    n = b.shape[1]
    tm, tn = _tile(m, tm), _tile(n, tn)
    return pl.pallas_call(
        _mm_kernel,
        grid=(m // tm, n // tn),
        in_specs=[pl.BlockSpec((tm, k), lambda i, j: (i, 0)),
                  pl.BlockSpec((k, tn), lambda i, j: (0, j))],
        out_specs=pl.BlockSpec((tm, tn), lambda i, j: (i, j)),
        out_shape=jax.ShapeDtypeStruct((m, n), out_dtype),
        compiler_params=_params("parallel", "parallel"),
        name="matmul",
    )(a, b)


def _layer_norm_rows(z, g, b):
    mu = jnp.mean(z, axis=-1, keepdims=True)
    zc = z - mu
    var = jnp.mean(zc * zc, axis=-1, keepdims=True)
    return zc * lax.rsqrt(var + LN_EPS) * g + b


def _proj_ln_kernel(a_ref, w_ref, x_ref, g_ref, b_ref, of_ref, ob_ref):
    y = jnp.dot(a_ref[...], w_ref[...], preferred_element_type=F32)
    out = _layer_norm_rows(ALPHA * x_ref[...] + y, g_ref[...], b_ref[...])
    of_ref[...] = out
    ob_ref[...] = out.astype(BF16)


def _proj_residual_ln(a, w, x, g, b, tm=256):
    s, k = a.shape
    d = w.shape[1]
    tm = _tile(s, tm)
    row = lambda i: (i, 0)
    fixed = lambda i: (0, 0)
    return pl.pallas_call(
        _proj_ln_kernel,
        grid=(s // tm,),
        in_specs=[pl.BlockSpec((tm, k), row), pl.BlockSpec((k, d), fixed),
                  pl.BlockSpec((tm, d), row), pl.BlockSpec((1, d), fixed),
                  pl.BlockSpec((1, d), fixed)],
        out_specs=[pl.BlockSpec((tm, d), row), pl.BlockSpec((tm, d), row)],
        out_shape=[jax.ShapeDtypeStruct((s, d), F32), jax.ShapeDtypeStruct((s, d), BF16)],
        compiler_params=_params("parallel"),
        name="proj_residual_ln",
    )(a, w, x, g.reshape(1, d), b.reshape(1, d))


A_Q_COLS = A_HEADS * A_HEAD_DIM
A_PAIR_COLS = 2 * A_HEAD_DIM


def _swa_weight(w_qkv):
    d = w_qkv.shape[0]
    wq = w_qkv[:, :A_Q_COLS]
    wk = w_qkv[:, A_Q_COLS:A_Q_COLS + A_KV_HEADS * A_HEAD_DIM].reshape(d, A_KV_HEADS, A_HEAD_DIM)
    wv = w_qkv[:, A_Q_COLS + A_KV_HEADS * A_HEAD_DIM:].reshape(d, A_KV_HEADS, A_HEAD_DIM)
    dup = lambda w: jnp.concatenate([w, w], axis=-1).reshape(d, A_KV_HEADS * A_PAIR_COLS)
    return jnp.concatenate([wq, dup(wk), dup(wv)], axis=1).astype(BF16)


def _half_masks(dtype):
    lane = lax.broadcasted_iota(jnp.int32, (1, LANES), 1)
    lo = (lane < LANES // 2).astype(F32)
    return lo.astype(dtype), (1.0 - lo).astype(dtype)


def _swa_kernel(q_ref, kp_ref, kc_ref, vp_ref, vc_ref, sink_ref, slope_ref, o_ref):
    i = pl.program_id(0)
    blk = WINDOW
    row = lax.broadcasted_iota(jnp.int32, (2 * blk, 2 * blk), 0)
    col = lax.broadcasted_iota(jnp.int32, (2 * blk, 2 * blk), 1)
    dist = jnp.bitwise_and(row, blk - 1) + blk - col
    valid = (dist >= 0) & (dist < WINDOW) & ((col >= blk) | (i > 0))
    distf = dist.astype(F32)
    top = row < blk
    top_col = top[:, :1]
    mlo, mhi = _half_masks(BF16)
    lo_f = lax.broadcasted_iota(jnp.int32, (blk, LANES), 1) < LANES // 2
    for g in range(A_KV_HEADS):
        kv = slice(g * A_PAIR_COLS, (g + 1) * A_PAIR_COLS)
        kband = jnp.concatenate([kp_ref[:, kv], kc_ref[:, kv]], axis=0)
        vband = jnp.concatenate([vp_ref[:, kv], vc_ref[:, kv]], axis=0)
        for p in range(A_GROUP // 2):
            h0 = g * A_GROUP + 2 * p
            qcols = slice((h0 // 2) * LANES, (h0 // 2 + 1) * LANES)
            q2 = q_ref[:, qcols]
            lhs = jnp.concatenate([q2 * mlo, q2 * mhi], axis=0)
            s = lax.dot_general(lhs, kband, (((1,), (1,)), ((), ())),
                                preferred_element_type=F32) * (A_HEAD_DIM ** -0.5)
            slope = jnp.where(top, slope_ref[h0], slope_ref[h0 + 1])
            sink = jnp.where(top_col, sink_ref[h0], sink_ref[h0 + 1])
            s = s - slope * distf
            s = jnp.where(valid, s, NEG)
            m = jnp.maximum(jnp.max(s, axis=-1, keepdims=True), sink)
            pexp = jnp.exp(s - m)
            denom = jnp.sum(pexp, axis=-1, keepdims=True) + jnp.exp(sink - m)
            probs = (pexp / denom).astype(BF16)
            o2 = jnp.dot(probs, vband, preferred_element_type=F32)
            o_ref[:, qcols] = jnp.where(lo_f, o2[:blk], o2[blk:]).astype(o_ref.dtype)


def _swa_attention(qkv, sinks):
    s = qkv.shape[0]
    nb = s // WINDOW
    prev = lambda i: jnp.maximum(i - 1, 0)
    smem = pl.BlockSpec(memory_space=pltpu.SMEM)
    kv_cols = A_KV_HEADS * A_PAIR_COLS
    k_blk, v_blk = A_Q_COLS // kv_cols, A_Q_COLS // kv_cols + 1
    return pl.pallas_call(
        _swa_kernel,
        grid=(nb,),
        in_specs=[pl.BlockSpec((WINDOW, A_Q_COLS), lambda i: (i, 0)),
                  pl.BlockSpec((WINDOW, kv_cols), lambda i: (prev(i), k_blk)),
                  pl.BlockSpec((WINDOW, kv_cols), lambda i: (i, k_blk)),
                  pl.BlockSpec((WINDOW, kv_cols), lambda i: (prev(i), v_blk)),
                  pl.BlockSpec((WINDOW, kv_cols), lambda i: (i, v_blk)),
                  smem, smem],
        out_specs=pl.BlockSpec((WINDOW, A_Q_COLS), lambda i: (i, 0)),
        out_shape=jax.ShapeDtypeStruct((s, A_Q_COLS), BF16),
        compiler_params=_params("parallel"),
        name="swa_attention",
    )(qkv, qkv, qkv, qkv, qkv, sinks.astype(F32), _alibi_slopes(A_HEADS))


B_QK_COLS = B_HEADS * 2 * B_HEAD_DIM
B_K_BLOCK0 = B_QK_COLS // B_V_DIM
B_V_BLOCK0 = 2 * B_K_BLOCK0


_ALIBI_COLS = 12
_POS_LOW = 256
_EXP_ZERO_BELOW = 110.0
_BOUND_SLACK = 1.001
_BOUND_MARGIN = 1.0


def _diff_head_tables(slope, lhs_scr, kx_scr, vx_scr, tq):
    sv = jnp.full((1, LANES), slope, F32)
    s1 = sv.astype(BF16).astype(F32)
    s2 = (sv - s1).astype(BF16).astype(F32)
    s3 = (sv - s1 - s2).astype(BF16).astype(F32)
    lane = lax.broadcasted_iota(jnp.int32, (1, LANES), 1)
    third = lambda k: (lane == k) | (lane == k + 3) | (lane == k + 6) | (lane == k + 9)
    piece = jnp.where(third(0), s1, jnp.where(third(1), s2, s3))

    row = lax.broadcasted_iota(jnp.int32, (2 * tq, LANES), 0)
    lane_q = lax.broadcasted_iota(jnp.int32, (2 * tq, LANES), 1)
    r = jnp.where(row >= tq, row - tq, row)
    r_lo = jnp.bitwise_and(r, _POS_LOW - 1).astype(F32)
    r_hi = jnp.where(r >= _POS_LOW, 1.0, 0.0)
    ext = jnp.where(lane_q < 3, r_lo, jnp.where(lane_q < 6, r_hi,
                    jnp.where(lane_q < _ALIBI_COLS, piece, 0.0)))
    lhs_scr[:, LANES:2 * LANES] = ext.astype(BF16)

    c = lax.broadcasted_iota(jnp.int32, (tq, LANES), 0)
    lane_k = lax.broadcasted_iota(jnp.int32, (tq, LANES), 1)
    c_lo = jnp.bitwise_and(c, _POS_LOW - 1).astype(F32)
    c_hi = jnp.where(c >= _POS_LOW, float(_POS_LOW), 0.0)
    kext = jnp.where(lane_k < 3, -piece, jnp.where(lane_k < 6, -float(_POS_LOW) * piece,
                     jnp.where(lane_k < 9, c_lo, jnp.where(lane_k < _ALIBI_COLS, c_hi, 0.0))))
    kx_scr[...] = kext.astype(BF16)
    vx_scr[...] = jnp.where(lane_k == 0, 1.0, 0.0).astype(BF16)


def _diff_kernel(q_ref, k_ref, v_ref, lq1_ref, lk1_ref, lq2_ref, lk2_ref, g_ref, slope_ref,
                 o_ref, lhs_scr, kx_scr, vx_scr, m_scr, acc_scr, s_scr, k2max_scr, *, tq,
                 lambda_init):
    h = pl.program_id(0)
    i = pl.program_id(1)
    slope = slope_ref[h]

    @pl.when(i == 0)
    def _head_init():
        _diff_head_tables(slope, lhs_scr, kx_scr, vx_scr, tq)
        k2 = jnp.square(k_ref[...].astype(F32))
        k2max = jnp.maximum(*[jnp.max(jnp.sum(k2 * msk, axis=-1, keepdims=True), axis=0,
                                      keepdims=True) for msk in _half_masks(F32)])
        k2max_scr[...] = jnp.broadcast_to(k2max, k2max_scr.shape)

    mlo, mhi = _half_masks(BF16)
    q = q_ref[...] * (B_HEAD_DIM ** -0.5)
    lhs_scr[0:tq, 0:LANES] = q * mlo
    lhs_scr[tq:2 * tq, 0:LANES] = q * mhi
    m_scr[...] = jnp.full(m_scr.shape, -jnp.inf, F32)
    acc_scr[...] = jnp.zeros(acc_scr.shape, F32)

    def scores(j, slot):
        start = pl.multiple_of(j * tq, tq)
        kx = jnp.concatenate([k_ref[pl.ds(start, tq), :], kx_scr[...]], axis=1)
        s_scr[slot] = lax.dot_general(lhs_scr[...], kx, (((1,), (1,)), ((), ())),
                                      preferred_element_type=F32)

    def accumulate(j, slot, diagonal):
        start = pl.multiple_of(j * tq, tq)
        vx = jnp.concatenate([v_ref[pl.ds(start, tq), :], vx_scr[...]], axis=1)
        for half in range(2):
            rows = slice(half * tq, (half + 1) * tq)
            s = s_scr[slot, rows, :]
            if diagonal:
                row = lax.broadcasted_iota(jnp.int32, (tq, tq), 0)
                col = lax.broadcasted_iota(jnp.int32, (tq, tq), 1)
                s = jnp.where(row >= col, s, NEG)
                off = 0.0
            else:
                off = slope * ((i - j) * tq).astype(F32)
            m_old = m_scr[rows, :]
            m_new = jnp.maximum(m_old, jnp.max(s, axis=-1, keepdims=True) - off)
            p = jnp.exp(s - (m_new + off))
            alpha = jnp.exp(m_old - m_new)
            acc_scr[rows, :] = alpha * acc_scr[rows, :] + jnp.dot(
                p.astype(BF16), vx, preferred_element_type=F32)
            m_scr[rows, :] = m_new

    scores(i, 0)
    scores(jnp.maximum(i - 1, 0), 1)
    accumulate(i, 0, True)
    qf = (q * mlo).astype(F32), (q * mhi).astype(F32)
    q2max = jnp.maximum(*[jnp.max(jnp.sum(c * c, axis=-1, keepdims=True), axis=0, keepdims=True)
                          for c in qf])
    m_min = jnp.min(m_scr[...], axis=0, keepdims=True)
    reach = (jnp.sqrt(q2max * k2max_scr[0:1, 0:1]) * _BOUND_SLACK + slope * (tq - 1)
             - m_min + _EXP_ZERO_BELOW + _BOUND_MARGIN) / (slope * tq)
    n_far = jnp.minimum(reach, i.astype(F32)).astype(jnp.int32)[0, 0]

    def pair(jj, carry):
        j = i - 1 - 2 * jj
        scores(j - 1, 0)
        accumulate(j, 1, False)
        scores(jnp.maximum(j - 2, 0), 1)
        accumulate(j - 1, 0, False)
        return carry

    lax.fori_loop(0, n_far // 2, pair, 0)

    @pl.when(n_far % 2 == 1)
    def _odd_tail():
        accumulate(i - n_far, 1, False)

    lam = (jnp.exp(jnp.sum(lq1_ref[...] * lk1_ref[...], axis=-1, keepdims=True))
           - jnp.exp(jnp.sum(lq2_ref[...] * lk2_ref[...], axis=-1, keepdims=True))
           + lambda_init)
    o0 = acc_scr[0:tq, 0:B_V_DIM] / acc_scr[0:tq, B_V_DIM:B_V_DIM + 1]
    o1 = acc_scr[tq:2 * tq, 0:B_V_DIM] / acc_scr[tq:2 * tq, B_V_DIM:B_V_DIM + 1]
    o = o0 - lam * o1
    o = o * lax.rsqrt(jnp.mean(o * o, axis=-1, keepdims=True) + LN_EPS)
    o = o * g_ref[...] * (1.0 - lambda_init)
    o_ref[...] = o.astype(o_ref.dtype)


def _diff_attention(qkv, lq1, lk1, lq2, lk2, subln_g, lambda_init, tq=512):
    s = qkv.shape[0]
    tq = _tile(s, tq)
    assert tq <= 2 * _POS_LOW and tq % _POS_LOW == 0, tq
    vec = lambda n: pl.BlockSpec((1, n), lambda h, i: (0, 0))
    kern = functools.partial(_diff_kernel, tq=tq, lambda_init=lambda_init)
    return pl.pallas_call(
        kern,
        grid=(B_HEADS, s // tq),
        in_specs=[pl.BlockSpec((tq, B_V_DIM), lambda h, i: (i, h)),
                  pl.BlockSpec((s, B_V_DIM), lambda h, i: (0, B_K_BLOCK0 + h)),
                  pl.BlockSpec((s, B_V_DIM), lambda h, i: (0, B_V_BLOCK0 + h)),
                  vec(B_HEAD_DIM), vec(B_HEAD_DIM), vec(B_HEAD_DIM), vec(B_HEAD_DIM),
                  vec(B_V_DIM), pl.BlockSpec(memory_space=pltpu.SMEM)],
        out_specs=pl.BlockSpec((tq, B_V_DIM), lambda h, i: (i, h)),
        out_shape=jax.ShapeDtypeStruct((s, B_HEADS * B_V_DIM), BF16),
        scratch_shapes=[pltpu.VMEM((2 * tq, 2 * LANES), BF16), pltpu.VMEM((tq, LANES), BF16),
                        pltpu.VMEM((tq, LANES), BF16), pltpu.VMEM((2 * tq, 1), F32),
                        pltpu.VMEM((2 * tq, 2 * LANES), F32), pltpu.VMEM((2, 2 * tq, tq), F32),
                        pltpu.VMEM((8, LANES), F32)],
        compiler_params=_params("arbitrary", "arbitrary"),
        name="diff_attention",
    )(qkv, qkv, qkv, lq1.reshape(1, -1).astype(F32), lk1.reshape(1, -1).astype(F32),
      lq2.reshape(1, -1).astype(F32), lk2.reshape(1, -1).astype(F32),
      subln_g.reshape(1, -1).astype(F32), _alibi_slopes(B_HEADS))


_N_RANKED = P_TOPK + 1
_PAIR_RANKS = [(i, j) for i in range(_N_RANKED) for j in range(_N_RANKED)
               if (i + 1) * (j + 1) <= _N_RANKED]
_CAND_ROWS = -(-len(_PAIR_RANKS) // 8) * 8


def _top_values(work, count):
    out = []
    for _ in range(count):
        mx = jnp.max(work, axis=0, keepdims=True)
        out.append(mx)
        work = jnp.where(work == mx, -jnp.inf, work)
    return out


def _peer_score_kernel(q_ref, sk_ref, thr_ref, e1_ref, a2_ref, e2_ref, cand_scr):
    tt = q_ref.shape[0]
    cand_scr[...] = jnp.full(cand_scr.shape, -jnp.inf, F32)
    for h in range(P_HEADS):
        st = []
        for c in range(2):
            qs = q_ref[:, (2 * h + c) * P_HALF:(2 * h + c + 1) * P_HALF]
            st.append(lax.dot_general(sk_ref[h, c], qs, (((1,), (1,)), ((), ())),
                                      preferred_element_type=F32))
        a = _top_values(st[0], _N_RANKED)
        b = _top_values(st[1], _N_RANKED)
        for r, (i, j) in enumerate(_PAIR_RANKS):
            cand_scr[r:r + 1, :] = a[i] + b[j]
        best = _top_values(cand_scr[...], _N_RANKED)
        z = jnp.zeros((1, tt), F32)
        for v in best[:P_TOPK]:
            z = z + jnp.exp(v - best[0])
        cut = 0.5 * (best[P_TOPK - 1] + best[P_TOPK])
        thr_ref[h] = cut - st[0]
        a2_ref[h] = st[1]
        e1_ref[h] = jnp.exp(st[0] - a[0]) / z
        e2_ref[h] = jnp.exp(st[1] - b[0])


def _peer_scores(q, subkeys, tt=512):
    s = q.shape[0]
    tt = _tile(s, tt)
    big = jax.ShapeDtypeStruct((P_HEADS, N_KEYS, s), F32)
    big_spec = pl.BlockSpec((P_HEADS, N_KEYS, tt), lambda t: (0, 0, t))
    return pl.pallas_call(
        _peer_score_kernel,
        grid=(s // tt,),
        in_specs=[pl.BlockSpec((tt, q.shape[1]), lambda t: (t, 0)),
                  pl.BlockSpec(subkeys.shape, lambda t: (0, 0, 0, 0))],
        out_specs=[big_spec, big_spec, big_spec, big_spec],
        out_shape=[big, big, big, big],
        scratch_shapes=[pltpu.VMEM((_CAND_ROWS, tt), F32)],
        compiler_params=_params("parallel"),
        name="peer_scores",
    )(q, subkeys)


def _gelu(a):
    return 0.5 * a * (1.0 + lax.erf(a * np.float32(np.sqrt(0.5))))


def _peer_dense_kernel(x_ref, u_ref, v_ref, thr_ref, e1_ref, a2_ref, e2_ref, xres_ref, g_ref,
                       b_ref, of_ref, ob_ref, act_scr, h_scr, acc_scr, *, n_etiles):
    s = pl.program_id(0)
    _, te, tt = act_scr.shape
    e_prev = lax.rem(jnp.maximum(s - 1, 0), n_etiles)

    @pl.when(s == 0)
    def _first():
        act_scr[1] = jnp.zeros((te, tt), F32)

    @pl.when(e_prev == 0)
    def _init():
        acc_scr[...] = jnp.zeros(acc_scr.shape, F32)

    def body(write_slot, read_slot):
        act_scr[write_slot] = lax.dot_general(u_ref[...], x_ref[...], (((1,), (1,)), ((), ())),
                                              preferred_element_type=F32)
        for il in range(te // N_KEYS):
            rows = slice(il * N_KEYS, (il + 1) * N_KEYS)
            for lg in range(tt // LANES):
                cols = slice(lg * LANES, (lg + 1) * LANES)
                gate = jnp.zeros((N_KEYS, LANES), F32)
                for h in range(P_HEADS):
                    w = e2_ref[h, :, cols] * e1_ref[h, il:il + 1, cols]
                    gate = jnp.where(a2_ref[h, :, cols] > thr_ref[h, il:il + 1, cols],
                                     gate + w, gate)
                h_scr[rows, cols] = (gate * _gelu(act_scr[read_slot, rows, cols])).astype(BF16)
        acc_scr[...] += lax.dot_general(h_scr[...], v_ref[...], (((0,), (0,)), ((), ())),
                                        preferred_element_type=F32)

    @pl.when(lax.rem(s, 2) == 0)
    def _even():
        body(0, 1)

    @pl.when(lax.rem(s, 2) == 1)
    def _odd():
        body(1, 0)

    @pl.when((e_prev == n_etiles - 1) & (s > 0))
    def _finish():
        out = _layer_norm_rows(ALPHA * xres_ref[...] + acc_scr[...], g_ref[...], b_ref[...])
        of_ref[...] = out
        ob_ref[...] = out.astype(BF16)


def _peer_dense_ln(x_bf, u_bf, v_bf, thr, e1, a2, e2, x_res, g, b, tt=512, te=1024):
    s, d = x_bf.shape
    n_exp = u_bf.shape[0]
    tt, te = _tile(s, tt), _tile(n_exp, te)
    ne = n_exp // te
    n_items = (s // tt) * ne
    cur = lambda i: jnp.minimum(i, n_items - 1)
    prev = lambda i: jnp.maximum(i - 1, 0)
    rows_spec = pl.BlockSpec((P_HEADS, te // N_KEYS, tt),
                             lambda i: (0, prev(i) % ne, prev(i) // ne))
    once = pl.Buffered(1)
    full_spec = pl.BlockSpec((P_HEADS, N_KEYS, tt), lambda i: (0, 0, prev(i) // ne),
                             pipeline_mode=once)
    tok_spec = pl.BlockSpec((tt, d), lambda i: (prev(i) // ne, 0))
    res_spec = pl.BlockSpec((tt, d), lambda i: (prev(i) // ne, 0), pipeline_mode=once)
    vec_spec = pl.BlockSpec((1, d), lambda i: (0, 0))
    return pl.pallas_call(
        functools.partial(_peer_dense_kernel, n_etiles=ne),
        grid=(n_items + 1,),
        in_specs=[pl.BlockSpec((tt, d), lambda i: (cur(i) // ne, 0)),
                  pl.BlockSpec((te, d), lambda i: (cur(i) % ne, 0)),
                  pl.BlockSpec((te, d), lambda i: (prev(i) % ne, 0)),
                  rows_spec, rows_spec, full_spec, full_spec, res_spec, vec_spec, vec_spec],
        out_specs=[tok_spec, tok_spec],
        out_shape=[jax.ShapeDtypeStruct((s, d), F32), jax.ShapeDtypeStruct((s, d), BF16)],
        scratch_shapes=[pltpu.VMEM((2, te, tt), F32), pltpu.VMEM((te, tt), BF16),
                        pltpu.VMEM((tt, d), F32)],
        compiler_params=_params("arbitrary"),
        name="peer_dense",
    )(x_bf, u_bf, v_bf, thr, e1, a2, e2, x_res, g.reshape(1, d), b.reshape(1, d))


def _peer_ffn_ln(x_f, x_bf, w_q, subkeys, u, v, g, b):
    q = _matmul(x_bf, w_q.astype(BF16), BF16)
    thr, e1, a2, e2 = _peer_scores(q, subkeys.astype(BF16))
    return _peer_dense_ln(x_bf, u.astype(BF16), v.astype(BF16), thr, e1, a2, e2, x_f, g, b)


def kernel(x, a_w_qkv, a_sinks, a_w_o, b_w_qkv, b_lambda_q1, b_lambda_k1, b_lambda_q2,
           b_lambda_k2, b_subln_g, b_w_o, ln1_g, ln1_b, ln2_g, ln2_b,
           peer_w_q, peer_subkeys, peer_u, peer_v):
    bsz, seq, d = x.shape
    xf = x.reshape(bsz * seq, d).astype(F32)
    assert bsz == 1, "attention kernels index one sequence"
    xb = xf.astype(BF16)
    for i in range(DEPTH):
        j = i // 2
        if i % 2 == 0:
            qkv = _matmul(xb, _swa_weight(a_w_qkv[j]), BF16)
            mix = _swa_attention(qkv, a_sinks[j])
            w_o = a_w_o[j]
        else:
            lambda_init = 0.8 - 0.6 * float(np.exp(-0.3 * i))
            qkv = _matmul(xb, b_w_qkv[j].astype(BF16), BF16)
            mix = _diff_attention(qkv, b_lambda_q1[j], b_lambda_k1[j], b_lambda_q2[j],
                                  b_lambda_k2[j], b_subln_g[j], lambda_init)
            w_o = b_w_o[j]
        xf, xb = _proj_residual_ln(mix, w_o.astype(BF16), xf, ln1_g[i], ln1_b[i])
        xf, xb = _peer_ffn_ln(xf, xb, peer_w_q[i], peer_subkeys[i], peer_u[i], peer_v[i],
                              ln2_g[i], ln2_b[i])
    return xf.reshape(bsz, seq, d).astype(x.dtype)
```

```python
import functools

import jax
import jax.numpy as jnp
import numpy as np
from jax import lax
from jax.experimental import pallas as pl
from jax.experimental.pallas import tpu as pltpu

F32 = jnp.float32
BF16 = jnp.bfloat16

DEPTH = 4
A_HEADS, A_KV_HEADS, A_HEAD_DIM, WINDOW = 32, 4, 64, 128
A_GROUP = A_HEADS // A_KV_HEADS
B_HEADS, B_HEAD_DIM = 16, 64
B_V_DIM = 2 * B_HEAD_DIM
P_HEADS, N_KEYS, P_TOPK, P_HALF = 8, 128, 16, 128
LN_EPS = 1e-5
NEG = -1e30
ALPHA = (2.0 * DEPTH) ** 0.25

LANES = 128
V7X_VMEM_LIMIT_BYTES = 56 * 1024 * 1024


def _tile(n, pref):
    t = min(n, pref)
    assert n % t == 0, (n, t)
    return t


def _params(*sem, flags=None):
    return pltpu.CompilerParams(dimension_semantics=sem, vmem_limit_bytes=V7X_VMEM_LIMIT_BYTES,
                                flags=flags)


def _alibi_slopes(n):
    return jnp.asarray((2.0 ** (-8.0 * np.arange(1, n + 1) / n)).astype(np.float32))


def _mm_kernel(a_ref, b_ref, o_ref):
    o_ref[...] = jnp.dot(a_ref[...], b_ref[...], preferred_element_type=F32).astype(o_ref.dtype)


def _matmul(a, b, out_dtype, tm=1024, tn=512):
    m, k = a.shape
    n = b.shape[1]
    tm, tn = _tile(m, tm), _tile(n, tn)
    return pl.pallas_call(
        _mm_kernel,
        grid=(m // tm, n // tn),
        in_specs=[pl.BlockSpec((tm, k), lambda i, j: (i, 0)),
                  pl.BlockSpec((k, tn), lambda i, j: (0, j))],
        out_specs=pl.BlockSpec((tm, tn), lambda i, j: (i, j)),
        out_shape=jax.ShapeDtypeStruct((m, n), out_dtype),
        compiler_params=_params("parallel", "parallel"),
        name="matmul",
    )(a, b)


def _layer_norm_rows(z, g, b):
    mu = jnp.mean(z, axis=-1, keepdims=True)
    zc = z - mu
    var = jnp.mean(zc * zc, axis=-1, keepdims=True)
    return zc * lax.rsqrt(var + LN_EPS) * g + b


def _proj_ln_kernel(a_ref, w_ref, x_ref, g_ref, b_ref, of_ref, ob_ref):
    y = jnp.dot(a_ref[...], w_ref[...], preferred_element_type=F32)
    out = _layer_norm_rows(ALPHA * x_ref[...] + y, g_ref[...], b_ref[...])
    of_ref[...] = out
    ob_ref[...] = out.astype(BF16)


def _proj_residual_ln(a, w, x, g, b, tm=512):
    s, k = a.shape
    d = w.shape[1]
    tm = _tile(s, tm)
    row = lambda i: (i, 0)
    fixed = lambda i: (0, 0)
    return pl.pallas_call(
        _proj_ln_kernel,
        grid=(s // tm,),
        in_specs=[pl.BlockSpec((tm, k), row),
                  pl.BlockSpec((k, d), fixed, pipeline_mode=pl.Buffered(1)),
                  pl.BlockSpec((tm, d), row), pl.BlockSpec((1, d), fixed),
                  pl.BlockSpec((1, d), fixed)],
        out_specs=[pl.BlockSpec((tm, d), row), pl.BlockSpec((tm, d), row)],
        out_shape=[jax.ShapeDtypeStruct((s, d), F32), jax.ShapeDtypeStruct((s, d), BF16)],
        compiler_params=_params("parallel"),
        name="proj_residual_ln",
    )(a, w, x, g.reshape(1, d), b.reshape(1, d))


A_Q_COLS = A_HEADS * A_HEAD_DIM
A_PAIR_COLS = 2 * A_HEAD_DIM


def _swa_weight(w_qkv):
    d = w_qkv.shape[0]
    wq = w_qkv[:, :A_Q_COLS]
    wk = w_qkv[:, A_Q_COLS:A_Q_COLS + A_KV_HEADS * A_HEAD_DIM].reshape(d, A_KV_HEADS, A_HEAD_DIM)
    wv = w_qkv[:, A_Q_COLS + A_KV_HEADS * A_HEAD_DIM:].reshape(d, A_KV_HEADS, A_HEAD_DIM)
    dup = lambda w: jnp.concatenate([w, w], axis=-1).reshape(d, A_KV_HEADS * A_PAIR_COLS)
    return jnp.concatenate([wq, dup(wk), dup(wv)], axis=1).astype(BF16)


def _half_masks(dtype):
    lane = lax.broadcasted_iota(jnp.int32, (1, LANES), 1)
    lo = (lane < LANES // 2).astype(F32)
    return lo.astype(dtype), (1.0 - lo).astype(dtype)


def _swa_kernel(q_ref, kp_ref, kc_ref, vp_ref, vc_ref, sink_ref, slope_ref, o_ref):
    i = pl.program_id(0)
    blk = WINDOW
    row = lax.broadcasted_iota(jnp.int32, (2 * blk, 2 * blk), 0)
    col = lax.broadcasted_iota(jnp.int32, (2 * blk, 2 * blk), 1)
    dist = jnp.bitwise_and(row, blk - 1) + blk - col
    valid = (dist >= 0) & (dist < WINDOW) & ((col >= blk) | (i > 0))
    distf = dist.astype(F32)
    top = row < blk
    top_col = top[:, :1]
    mlo, mhi = _half_masks(BF16)
    lo_f = lax.broadcasted_iota(jnp.int32, (blk, LANES), 1) < LANES // 2
    for g in range(A_KV_HEADS):
        kv = slice(g * A_PAIR_COLS, (g + 1) * A_PAIR_COLS)
        kband = jnp.concatenate([kp_ref[:, kv], kc_ref[:, kv]], axis=0)
        vband = jnp.concatenate([vp_ref[:, kv], vc_ref[:, kv]], axis=0)
        for p in range(A_GROUP // 2):
            h0 = g * A_GROUP + 2 * p
            qcols = slice((h0 // 2) * LANES, (h0 // 2 + 1) * LANES)
            q2 = q_ref[:, qcols]
            lhs = jnp.concatenate([q2 * mlo, q2 * mhi], axis=0)
            s = lax.dot_general(lhs, kband, (((1,), (1,)), ((), ())),
                                preferred_element_type=F32) * (A_HEAD_DIM ** -0.5)
            slope = jnp.where(top, slope_ref[h0], slope_ref[h0 + 1])
            sink = jnp.where(top_col, sink_ref[h0], sink_ref[h0 + 1])
            s = s - slope * distf
            s = jnp.where(valid, s, NEG)
            m = jnp.maximum(jnp.max(s, axis=-1, keepdims=True), sink)
            pexp = jnp.exp(s - m)
            denom = jnp.sum(pexp, axis=-1, keepdims=True) + jnp.exp(sink - m)
            probs = (pexp / denom).astype(BF16)
            o2 = jnp.dot(probs, vband, preferred_element_type=F32)
            o_ref[:, qcols] = jnp.where(lo_f, o2[:blk], o2[blk:]).astype(o_ref.dtype)


def _swa_attention(qkv, sinks):
    s = qkv.shape[0]
    nb = s // WINDOW
    prev = lambda i: jnp.maximum(i - 1, 0)
    smem = pl.BlockSpec(memory_space=pltpu.SMEM)
    kv_cols = A_KV_HEADS * A_PAIR_COLS
    k_blk, v_blk = A_Q_COLS // kv_cols, A_Q_COLS // kv_cols + 1
    return pl.pallas_call(
        _swa_kernel,
        grid=(nb,),
        in_specs=[pl.BlockSpec((WINDOW, A_Q_COLS), lambda i: (i, 0)),
                  pl.BlockSpec((WINDOW, kv_cols), lambda i: (prev(i), k_blk)),
                  pl.BlockSpec((WINDOW, kv_cols), lambda i: (i, k_blk)),
                  pl.BlockSpec((WINDOW, kv_cols), lambda i: (prev(i), v_blk)),
                  pl.BlockSpec((WINDOW, kv_cols), lambda i: (i, v_blk)),
                  smem, smem],
        out_specs=pl.BlockSpec((WINDOW, A_Q_COLS), lambda i: (i, 0)),
        out_shape=jax.ShapeDtypeStruct((s, A_Q_COLS), BF16),
        compiler_params=_params("parallel"),
        name="swa_attention",
    )(qkv, qkv, qkv, qkv, qkv, sinks.astype(F32), _alibi_slopes(A_HEADS))


B_QK_COLS = B_HEADS * 2 * B_HEAD_DIM
B_K_BLOCK0 = B_QK_COLS // B_V_DIM
B_V_BLOCK0 = 2 * B_K_BLOCK0


_ALIBI_COLS = 12
_POS_LOW = 256
_EXP_ZERO_BELOW = 110.0
_BOUND_SLACK = 1.001
_BOUND_MARGIN = 1.0


def _diff_head_tables(slope, lhs_scr, kx_scr, vx_scr, tq):
    sv = jnp.full((1, LANES), slope, F32)
    s1 = sv.astype(BF16).astype(F32)
    s2 = (sv - s1).astype(BF16).astype(F32)
    s3 = (sv - s1 - s2).astype(BF16).astype(F32)
    lane = lax.broadcasted_iota(jnp.int32, (1, LANES), 1)
    third = lambda k: (lane == k) | (lane == k + 3) | (lane == k + 6) | (lane == k + 9)
    piece = jnp.where(third(0), s1, jnp.where(third(1), s2, s3))

    row = lax.broadcasted_iota(jnp.int32, (2 * tq, LANES), 0)
    lane_q = lax.broadcasted_iota(jnp.int32, (2 * tq, LANES), 1)
    r = jnp.where(row >= tq, row - tq, row)
    r_lo = jnp.bitwise_and(r, _POS_LOW - 1).astype(F32)
    r_hi = jnp.where(r >= _POS_LOW, 1.0, 0.0)
    ext = jnp.where(lane_q < 3, r_lo, jnp.where(lane_q < 6, r_hi,
                    jnp.where(lane_q < _ALIBI_COLS, piece, 0.0)))
    lhs_scr[:, LANES:2 * LANES] = ext.astype(BF16)

    c = lax.broadcasted_iota(jnp.int32, (tq, LANES), 0)
    lane_k = lax.broadcasted_iota(jnp.int32, (tq, LANES), 1)
    c_lo = jnp.bitwise_and(c, _POS_LOW - 1).astype(F32)
    c_hi = jnp.where(c >= _POS_LOW, float(_POS_LOW), 0.0)
    kext = jnp.where(lane_k < 3, -piece, jnp.where(lane_k < 6, -float(_POS_LOW) * piece,
                     jnp.where(lane_k < 9, c_lo, jnp.where(lane_k < _ALIBI_COLS, c_hi, 0.0))))
    kx_scr[...] = kext.astype(BF16)
    vx_scr[...] = jnp.where(lane_k == 0, 1.0, 0.0).astype(BF16)


def _diff_kernel(q_ref, k_ref, v_ref, lq1_ref, lk1_ref, lq2_ref, lk2_ref, g_ref, slope_ref,
                 o_ref, lhs_scr, kx_scr, vx_scr, m_scr, acc_scr, s_scr, k2max_scr, *, tq,
                 lambda_init):
    h = pl.program_id(0)
    i = pl.program_id(1)
    slope = slope_ref[h]

    @pl.when(i == 0)
    def _head_init():
        _diff_head_tables(slope, lhs_scr, kx_scr, vx_scr, tq)
        k2 = jnp.square(k_ref[...].astype(F32))
        k2max = jnp.maximum(*[jnp.max(jnp.sum(k2 * msk, axis=-1, keepdims=True), axis=0,
                                      keepdims=True) for msk in _half_masks(F32)])
        k2max_scr[...] = jnp.broadcast_to(k2max, k2max_scr.shape)

    mlo, mhi = _half_masks(BF16)
    q = q_ref[...] * (B_HEAD_DIM ** -0.5)
    lhs_scr[0:tq, 0:LANES] = q * mlo
    lhs_scr[tq:2 * tq, 0:LANES] = q * mhi
    m_scr[...] = jnp.full(m_scr.shape, -jnp.inf, F32)
    acc_scr[...] = jnp.zeros(acc_scr.shape, F32)

    def scores(j, slot):
        start = pl.multiple_of(j * tq, tq)
        kx = jnp.concatenate([k_ref[pl.ds(start, tq), :], kx_scr[...]], axis=1)
        s_scr[slot] = lax.dot_general(lhs_scr[...], kx, (((1,), (1,)), ((), ())),
                                      preferred_element_type=F32)

    def accumulate(j, slot, diagonal):
        start = pl.multiple_of(j * tq, tq)
        vx = jnp.concatenate([v_ref[pl.ds(start, tq), :], vx_scr[...]], axis=1)
        for half in range(2):
            rows = slice(half * tq, (half + 1) * tq)
            s = s_scr[slot, rows, :]
            if diagonal:
                row = lax.broadcasted_iota(jnp.int32, (tq, tq), 0)
                col = lax.broadcasted_iota(jnp.int32, (tq, tq), 1)
                s = jnp.where(row >= col, s, NEG)
                off = 0.0
            else:
                off = slope * ((i - j) * tq).astype(F32)
            m_old = m_scr[rows, :]
            m_new = jnp.maximum(m_old, jnp.max(s, axis=-1, keepdims=True) - off)
            p = jnp.exp(s - (m_new + off))
            alpha = jnp.exp(m_old - m_new)
            acc_scr[rows, :] = alpha * acc_scr[rows, :] + jnp.dot(
                p.astype(BF16), vx, preferred_element_type=F32)
            m_scr[rows, :] = m_new

    scores(i, 0)
    scores(jnp.maximum(i - 1, 0), 1)
    accumulate(i, 0, True)
    qf = (q * mlo).astype(F32), (q * mhi).astype(F32)
    q2max = jnp.maximum(*[jnp.max(jnp.sum(c * c, axis=-1, keepdims=True), axis=0, keepdims=True)
                          for c in qf])
    m_min = jnp.min(m_scr[...], axis=0, keepdims=True)
    reach = (jnp.sqrt(q2max * k2max_scr[0:1, 0:1]) * _BOUND_SLACK + slope * (tq - 1)
             - m_min + _EXP_ZERO_BELOW + _BOUND_MARGIN) / (slope * tq)
    n_far = jnp.minimum(reach, i.astype(F32)).astype(jnp.int32)[0, 0]

    def pair(jj, carry):
        j = i - 1 - 2 * jj
        scores(j - 1, 0)
        accumulate(j, 1, False)
        scores(jnp.maximum(j - 2, 0), 1)
        accumulate(j - 1, 0, False)
        return carry

    lax.fori_loop(0, n_far // 2, pair, 0)

    @pl.when(n_far % 2 == 1)
    def _odd_tail():
        accumulate(i - n_far, 1, False)

    lam = (jnp.exp(jnp.sum(lq1_ref[...] * lk1_ref[...], axis=-1, keepdims=True))
           - jnp.exp(jnp.sum(lq2_ref[...] * lk2_ref[...], axis=-1, keepdims=True))
           + lambda_init)
    o0 = acc_scr[0:tq, 0:B_V_DIM] / acc_scr[0:tq, B_V_DIM:B_V_DIM + 1]
    o1 = acc_scr[tq:2 * tq, 0:B_V_DIM] / acc_scr[tq:2 * tq, B_V_DIM:B_V_DIM + 1]
    o = o0 - lam * o1
    o = o * lax.rsqrt(jnp.mean(o * o, axis=-1, keepdims=True) + LN_EPS)
    o = o * g_ref[...] * (1.0 - lambda_init)
    o_ref[...] = o.astype(o_ref.dtype)


def _diff_attention(qkv, lq1, lk1, lq2, lk2, subln_g, lambda_init, tq=512):
    s = qkv.shape[0]
    tq = _tile(s, tq)
    assert tq <= 2 * _POS_LOW and tq % _POS_LOW == 0, tq
    vec = lambda n: pl.BlockSpec((1, n), lambda h, i: (0, 0))
    kern = functools.partial(_diff_kernel, tq=tq, lambda_init=lambda_init)
    return pl.pallas_call(
        kern,
        grid=(B_HEADS, s // tq),
        in_specs=[pl.BlockSpec((tq, B_V_DIM), lambda h, i: (i, h)),
                  pl.BlockSpec((s, B_V_DIM), lambda h, i: (0, B_K_BLOCK0 + h)),
                  pl.BlockSpec((s, B_V_DIM), lambda h, i: (0, B_V_BLOCK0 + h)),
                  vec(B_HEAD_DIM), vec(B_HEAD_DIM), vec(B_HEAD_DIM), vec(B_HEAD_DIM),
                  vec(B_V_DIM), pl.BlockSpec(memory_space=pltpu.SMEM)],
        out_specs=pl.BlockSpec((tq, B_V_DIM), lambda h, i: (i, h)),
        out_shape=jax.ShapeDtypeStruct((s, B_HEADS * B_V_DIM), BF16),
        scratch_shapes=[pltpu.VMEM((2 * tq, 2 * LANES), BF16), pltpu.VMEM((tq, LANES), BF16),
                        pltpu.VMEM((tq, LANES), BF16), pltpu.VMEM((2 * tq, 1), F32),
                        pltpu.VMEM((2 * tq, 2 * LANES), F32), pltpu.VMEM((2, 2 * tq, tq), F32),
                        pltpu.VMEM((8, LANES), F32)],
        compiler_params=_params("arbitrary", "arbitrary"),
        name="diff_attention",
    )(qkv, qkv, qkv, lq1.reshape(1, -1).astype(F32), lk1.reshape(1, -1).astype(F32),
      lq2.reshape(1, -1).astype(F32), lk2.reshape(1, -1).astype(F32),
      subln_g.reshape(1, -1).astype(F32), _alibi_slopes(B_HEADS))


_N_RANKED = P_TOPK + 1
_PAIR_RANKS = [(i, j) for i in range(_N_RANKED) for j in range(_N_RANKED)
               if (i + 1) * (j + 1) <= _N_RANKED]
_CAND_ROWS = -(-len(_PAIR_RANKS) // 8) * 8


def _top_values(work, count):
    out = []
    for _ in range(count):
        mx = jnp.max(work, axis=0, keepdims=True)
        out.append(mx)
        work = jnp.where(work == mx, -jnp.inf, work)
    return out


def _peer_score_kernel(q_ref, sk_ref, thr_ref, e1_ref, a2_ref, e2_ref, cand_scr):
    tt = q_ref.shape[0]
    cand_scr[...] = jnp.full(cand_scr.shape, -jnp.inf, F32)
    for h in range(P_HEADS):
        st = []
        for c in range(2):
            qs = q_ref[:, (2 * h + c) * P_HALF:(2 * h + c + 1) * P_HALF]
            st.append(lax.dot_general(sk_ref[h, c], qs, (((1,), (1,)), ((), ())),
                                      preferred_element_type=F32))
        a = _top_values(st[0], _N_RANKED)
        b = _top_values(st[1], _N_RANKED)
        for r, (i, j) in enumerate(_PAIR_RANKS):
            cand_scr[r:r + 1, :] = a[i] + b[j]
        best = _top_values(cand_scr[...], _N_RANKED)
        z = jnp.zeros((1, tt), F32)
        for v in best[:P_TOPK]:
            z = z + jnp.exp(v - best[0])
        cut = 0.5 * (best[P_TOPK - 1] + best[P_TOPK])
        thr_ref[h] = cut - st[0]
        a2_ref[h] = st[1]
        e1_ref[h] = jnp.exp(st[0] - a[0]) / z
        e2_ref[h] = jnp.exp(st[1] - b[0])


def _peer_scores(q, subkeys, tt=512):
    s = q.shape[0]
    tt = _tile(s, tt)
    big = jax.ShapeDtypeStruct((P_HEADS, N_KEYS, s), F32)
    big_spec = pl.BlockSpec((P_HEADS, N_KEYS, tt), lambda t: (0, 0, t))
    return pl.pallas_call(
        _peer_score_kernel,
        grid=(s // tt,),
        in_specs=[pl.BlockSpec((tt, q.shape[1]), lambda t: (t, 0)),
                  pl.BlockSpec(subkeys.shape, lambda t: (0, 0, 0, 0))],
        out_specs=[big_spec, big_spec, big_spec, big_spec],
        out_shape=[big, big, big, big],
        scratch_shapes=[pltpu.VMEM((_CAND_ROWS, tt), F32)],
        compiler_params=_params("parallel"),
        name="peer_scores",
    )(q, subkeys)


def _gelu(a):
    return 0.5 * a * (1.0 + lax.erf(a * np.float32(np.sqrt(0.5))))


def _peer_dense_kernel(x_ref, u_ref, v_ref, thr_ref, e1_ref, a2_ref, e2_ref, xres_ref, g_ref,
                       b_ref, of_ref, ob_ref, act_scr, h_scr, acc_scr, *, n_etiles):
    s = pl.program_id(0)
    _, te, tt = act_scr.shape
    e_prev = lax.rem(jnp.maximum(s - 1, 0), n_etiles)

    @pl.when(s == 0)
    def _first():
        act_scr[1] = jnp.zeros((te, tt), F32)

    @pl.when(e_prev == 0)
    def _init():
        acc_scr[...] = jnp.zeros(acc_scr.shape, F32)

    def body(write_slot, read_slot):
        act_scr[write_slot] = lax.dot_general(u_ref[...], x_ref[...], (((1,), (1,)), ((), ())),
                                              preferred_element_type=F32)
        for il in range(te // N_KEYS):
            rows = slice(il * N_KEYS, (il + 1) * N_KEYS)
            for lg in range(tt // LANES):
                cols = slice(lg * LANES, (lg + 1) * LANES)
                gate = jnp.zeros((N_KEYS, LANES), F32)
                for h in range(P_HEADS):
                    w = e2_ref[h, :, cols] * e1_ref[h, il:il + 1, cols]
                    gate = jnp.where(a2_ref[h, :, cols] > thr_ref[h, il:il + 1, cols],
                                     gate + w, gate)
                h_scr[rows, cols] = (gate * _gelu(act_scr[read_slot, rows, cols])).astype(BF16)
        acc_scr[...] += jnp.dot(v_ref[...], h_scr[...], preferred_element_type=F32)

    @pl.when(lax.rem(s, 2) == 0)
    def _even():
        body(0, 1)

    @pl.when(lax.rem(s, 2) == 1)
    def _odd():
        body(1, 0)

    @pl.when((e_prev == n_etiles - 1) & (s > 0))
    def _finish():
        out = _layer_norm_rows(ALPHA * xres_ref[...] + acc_scr[...].T, g_ref[...], b_ref[...])
        of_ref[...] = out
        ob_ref[...] = out.astype(BF16)


def _peer_dense_ln(x_bf, u_bf, vt_bf, thr, e1, a2, e2, x_res, g, b, tt=512, te=1024):
    s, d = x_bf.shape
    n_exp = u_bf.shape[0]
    tt, te = _tile(s, tt), _tile(n_exp, te)
    ne = n_exp // te
    n_items = (s // tt) * ne
    cur = lambda i: jnp.minimum(i, n_items - 1)
    prev = lambda i: jnp.maximum(i - 1, 0)
    rows_spec = pl.BlockSpec((P_HEADS, te // N_KEYS, tt),
                             lambda i: (0, prev(i) % ne, prev(i) // ne))
    once = pl.Buffered(1)
    full_spec = pl.BlockSpec((P_HEADS, N_KEYS, tt), lambda i: (0, 0, prev(i) // ne),
                             pipeline_mode=once)
    tok_spec = pl.BlockSpec((tt, d), lambda i: (prev(i) // ne, 0))
    res_spec = pl.BlockSpec((tt, d), lambda i: (prev(i) // ne, 0), pipeline_mode=once)
    vec_spec = pl.BlockSpec((1, d), lambda i: (0, 0))
    return pl.pallas_call(
        functools.partial(_peer_dense_kernel, n_etiles=ne),
        grid=(n_items + 1,),
        in_specs=[pl.BlockSpec((tt, d), lambda i: (cur(i) // ne, 0)),
                  pl.BlockSpec((te, d), lambda i: (cur(i) % ne, 0)),
                  pl.BlockSpec((d, te), lambda i: (0, prev(i) % ne)),
                  rows_spec, rows_spec, full_spec, full_spec, res_spec, vec_spec, vec_spec],
        out_specs=[tok_spec, tok_spec],
        out_shape=[jax.ShapeDtypeStruct((s, d), F32), jax.ShapeDtypeStruct((s, d), BF16)],
        scratch_shapes=[pltpu.VMEM((2, te, tt), F32), pltpu.VMEM((te, tt), BF16),
                        pltpu.VMEM((d, tt), F32)],
        compiler_params=_params("arbitrary"),
        name="peer_dense",
    )(x_bf, u_bf, vt_bf, thr, e1, a2, e2, x_res, g.reshape(1, d), b.reshape(1, d))


def _peer_ffn_ln(x_f, x_bf, w_q, subkeys, u, v, g, b):
    q = _matmul(x_bf, w_q.astype(BF16), BF16)
    thr, e1, a2, e2 = _peer_scores(q, subkeys.astype(BF16))
    return _peer_dense_ln(x_bf, u.astype(BF16), v.astype(BF16).T, thr, e1, a2, e2, x_f, g, b)


def kernel(x, a_w_qkv, a_sinks, a_w_o, b_w_qkv, b_lambda_q1, b_lambda_k1, b_lambda_q2,
           b_lambda_k2, b_subln_g, b_w_o, ln1_g, ln1_b, ln2_g, ln2_b,
           peer_w_q, peer_subkeys, peer_u, peer_v):
    bsz, seq, d = x.shape
    xf = x.reshape(bsz * seq, d).astype(F32)
    assert bsz == 1, "attention kernels index one sequence"
    xb = xf.astype(BF16)
    for i in range(DEPTH):
        j = i // 2
        if i % 2 == 0:
            qkv = _matmul(xb, _swa_weight(a_w_qkv[j]), BF16)
            mix = _swa_attention(qkv, a_sinks[j])
            w_o = a_w_o[j]
        else:
            lambda_init = 0.8 - 0.6 * float(np.exp(-0.3 * i))
            qkv = _matmul(xb, b_w_qkv[j].astype(BF16), BF16)
            mix = _diff_attention(qkv, b_lambda_q1[j], b_lambda_k1[j], b_lambda_q2[j],
                                  b_lambda_k2[j], b_subln_g[j], lambda_init)
            w_o = b_w_o[j]
        xf, xb = _proj_residual_ln(mix, w_o.astype(BF16), xf, ln1_g[i], ln1_b[i])
        xf, xb = _peer_ffn_ln(xf, xb, peer_w_q[i], peer_subkeys[i], peer_u[i], peer_v[i],
                              ln2_g[i], ln2_b[i])
    return xf.reshape(bsz, seq, d).astype(x.dtype)
```

```python
import functools

import jax
import jax.numpy as jnp
import numpy as np
from jax import lax
from jax.experimental import pallas as pl
from jax.experimental.pallas import tpu as pltpu

F32 = jnp.float32
BF16 = jnp.bfloat16

DEPTH = 4
A_HEADS, A_KV_HEADS, A_HEAD_DIM, WINDOW = 32, 4, 64, 128
A_GROUP = A_HEADS // A_KV_HEADS
B_HEADS, B_HEAD_DIM = 16, 64
B_V_DIM = 2 * B_HEAD_DIM
P_HEADS, N_KEYS, P_TOPK, P_HALF = 8, 128, 16, 128
LN_EPS = 1e-5
NEG = -1e30
ALPHA = (2.0 * DEPTH) ** 0.25

LANES = 128
V7X_VMEM_LIMIT_BYTES = 56 * 1024 * 1024


def _tile(n, pref):
    t = min(n, pref)
    assert n % t == 0, (n, t)
    return t


def _params(*sem, flags=None):
    return pltpu.CompilerParams(dimension_semantics=sem, vmem_limit_bytes=V7X_VMEM_LIMIT_BYTES,
                                flags=flags)


def _alibi_slopes(n):
    return jnp.asarray((2.0 ** (-8.0 * np.arange(1, n + 1) / n)).astype(np.float32))


def _mm_kernel(a_ref, b_ref, o_ref):
    o_ref[...] = jnp.dot(a_ref[...], b_ref[...], preferred_element_type=F32).astype(o_ref.dtype)


def _matmul(a, b, out_dtype, tm=1024, tn=512):
    m, k = a.shape
    n = b.shape[1]
    tm, tn = _tile(m, tm), _tile(n, tn)
    return pl.pallas_call(
        _mm_kernel,
        grid=(m // tm, n // tn),
        in_specs=[pl.BlockSpec((tm, k), lambda i, j: (i, 0)),
                  pl.BlockSpec((k, tn), lambda i, j: (0, j))],
        out_specs=pl.BlockSpec((tm, tn), lambda i, j: (i, j)),
        out_shape=jax.ShapeDtypeStruct((m, n), out_dtype),
        compiler_params=_params("parallel", "parallel"),
        name="matmul",
    )(a, b)


def _layer_norm_rows(z, g, b):
    mu = jnp.mean(z, axis=-1, keepdims=True)
    zc = z - mu
    var = jnp.mean(zc * zc, axis=-1, keepdims=True)
    return zc * lax.rsqrt(var + LN_EPS) * g + b


def _proj_ln_kernel(a_ref, w_ref, x_ref, g_ref, b_ref, of_ref, ob_ref):
    y = jnp.dot(a_ref[...], w_ref[...], preferred_element_type=F32)
    out = _layer_norm_rows(ALPHA * x_ref[...] + y, g_ref[...], b_ref[...])
    of_ref[...] = out
    ob_ref[...] = out.astype(BF16)


def _proj_residual_ln(a, w, x, g, b, tm=256):
    s, k = a.shape
    d = w.shape[1]
    tm = _tile(s, tm)
    row = lambda i: (i, 0)
    fixed = lambda i: (0, 0)
    return pl.pallas_call(
        _proj_ln_kernel,
        grid=(s // tm,),
        in_specs=[pl.BlockSpec((tm, k), row), pl.BlockSpec((k, d), fixed),
                  pl.BlockSpec((tm, d), row), pl.BlockSpec((1, d), fixed),
                  pl.BlockSpec((1, d), fixed)],
        out_specs=[pl.BlockSpec((tm, d), row), pl.BlockSpec((tm, d), row)],
        out_shape=[jax.ShapeDtypeStruct((s, d), F32), jax.ShapeDtypeStruct((s, d), BF16)],
        compiler_params=_params("parallel"),
        name="proj_residual_ln",
    )(a, w, x, g.reshape(1, d), b.reshape(1, d))


A_Q_COLS = A_HEADS * A_HEAD_DIM
A_PAIR_COLS = 2 * A_HEAD_DIM


def _swa_weight(w_qkv):
    d = w_qkv.shape[0]
    wq = w_qkv[:, :A_Q_COLS]
    wk = w_qkv[:, A_Q_COLS:A_Q_COLS + A_KV_HEADS * A_HEAD_DIM].reshape(d, A_KV_HEADS, A_HEAD_DIM)
    wv = w_qkv[:, A_Q_COLS + A_KV_HEADS * A_HEAD_DIM:].reshape(d, A_KV_HEADS, A_HEAD_DIM)
    dup = lambda w: jnp.concatenate([w, w], axis=-1).reshape(d, A_KV_HEADS * A_PAIR_COLS)
    return jnp.concatenate([wq, dup(wk), dup(wv)], axis=1).astype(BF16)


def _half_masks(dtype):
    lane = lax.broadcasted_iota(jnp.int32, (1, LANES), 1)
    lo = (lane < LANES // 2).astype(F32)
    return lo.astype(dtype), (1.0 - lo).astype(dtype)


def _swa_kernel(q_ref, kp_ref, kc_ref, vp_ref, vc_ref, sink_ref, slope_ref, o_ref):
    i = pl.program_id(0)
    blk = WINDOW
    row = lax.broadcasted_iota(jnp.int32, (2 * blk, 2 * blk), 0)
    col = lax.broadcasted_iota(jnp.int32, (2 * blk, 2 * blk), 1)
    dist = jnp.bitwise_and(row, blk - 1) + blk - col
    valid = (dist >= 0) & (dist < WINDOW) & ((col >= blk) | (i > 0))
    distf = dist.astype(F32)
    top = row < blk
    top_col = top[:, :1]
    mlo, mhi = _half_masks(BF16)
    lo_f = lax.broadcasted_iota(jnp.int32, (blk, LANES), 1) < LANES // 2
    for g in range(A_KV_HEADS):
        kv = slice(g * A_PAIR_COLS, (g + 1) * A_PAIR_COLS)
        kband = jnp.concatenate([kp_ref[:, kv], kc_ref[:, kv]], axis=0)
        vband = jnp.concatenate([vp_ref[:, kv], vc_ref[:, kv]], axis=0)
        for p in range(A_GROUP // 2):
            h0 = g * A_GROUP + 2 * p
            qcols = slice((h0 // 2) * LANES, (h0 // 2 + 1) * LANES)
            q2 = q_ref[:, qcols]
            lhs = jnp.concatenate([q2 * mlo, q2 * mhi], axis=0)
            s = lax.dot_general(lhs, kband, (((1,), (1,)), ((), ())),
                                preferred_element_type=F32) * (A_HEAD_DIM ** -0.5)
            slope = jnp.where(top, slope_ref[h0], slope_ref[h0 + 1])
            sink = jnp.where(top_col, sink_ref[h0], sink_ref[h0 + 1])
            s = s - slope * distf
            s = jnp.where(valid, s, NEG)
            m = jnp.maximum(jnp.max(s, axis=-1, keepdims=True), sink)
            pexp = jnp.exp(s - m)
            denom = jnp.sum(pexp, axis=-1, keepdims=True) + jnp.exp(sink - m)
            probs = (pexp / denom).astype(BF16)
            o2 = jnp.dot(probs, vband, preferred_element_type=F32)
            o_ref[:, qcols] = jnp.where(lo_f, o2[:blk], o2[blk:]).astype(o_ref.dtype)


def _swa_attention(qkv, sinks):
    s = qkv.shape[0]
    nb = s // WINDOW
    prev = lambda i: jnp.maximum(i - 1, 0)
    smem = pl.BlockSpec(memory_space=pltpu.SMEM)
    kv_cols = A_KV_HEADS * A_PAIR_COLS
    k_blk, v_blk = A_Q_COLS // kv_cols, A_Q_COLS // kv_cols + 1
    return pl.pallas_call(
        _swa_kernel,
        grid=(nb,),
        in_specs=[pl.BlockSpec((WINDOW, A_Q_COLS), lambda i: (i, 0)),
                  pl.BlockSpec((WINDOW, kv_cols), lambda i: (prev(i), k_blk)),
                  pl.BlockSpec((WINDOW, kv_cols), lambda i: (i, k_blk)),
                  pl.BlockSpec((WINDOW, kv_cols), lambda i: (prev(i), v_blk)),
                  pl.BlockSpec((WINDOW, kv_cols), lambda i: (i, v_blk)),
                  smem, smem],
        out_specs=pl.BlockSpec((WINDOW, A_Q_COLS), lambda i: (i, 0)),
        out_shape=jax.ShapeDtypeStruct((s, A_Q_COLS), BF16),
        compiler_params=_params("parallel"),
        name="swa_attention",
    )(qkv, qkv, qkv, qkv, qkv, sinks.astype(F32), _alibi_slopes(A_HEADS))


B_QK_COLS = B_HEADS * 2 * B_HEAD_DIM
B_K_BLOCK0 = B_QK_COLS // B_V_DIM
B_V_BLOCK0 = 2 * B_K_BLOCK0


_ALIBI_COLS = 12
_POS_LOW = 256
_EXP_ZERO_BELOW = 110.0
_BOUND_SLACK = 1.001
_BOUND_MARGIN = 1.0


def _diff_head_tables(slope, lhs_scr, kx_scr, vx_scr, tq):
    sv = jnp.full((1, LANES), slope, F32)
    s1 = sv.astype(BF16).astype(F32)
    s2 = (sv - s1).astype(BF16).astype(F32)
    s3 = (sv - s1 - s2).astype(BF16).astype(F32)
    lane = lax.broadcasted_iota(jnp.int32, (1, LANES), 1)
    third = lambda k: (lane == k) | (lane == k + 3) | (lane == k + 6) | (lane == k + 9)
    piece = jnp.where(third(0), s1, jnp.where(third(1), s2, s3))

    row = lax.broadcasted_iota(jnp.int32, (2 * tq, LANES), 0)
    lane_q = lax.broadcasted_iota(jnp.int32, (2 * tq, LANES), 1)
    r = jnp.where(row >= tq, row - tq, row)
    r_lo = jnp.bitwise_and(r, _POS_LOW - 1).astype(F32)
    r_hi = jnp.where(r >= _POS_LOW, 1.0, 0.0)
    ext = jnp.where(lane_q < 3, r_lo, jnp.where(lane_q < 6, r_hi,
                    jnp.where(lane_q < _ALIBI_COLS, piece, 0.0)))
    lhs_scr[:, LANES:2 * LANES] = ext.astype(BF16)

    c = lax.broadcasted_iota(jnp.int32, (tq, LANES), 0)
    lane_k = lax.broadcasted_iota(jnp.int32, (tq, LANES), 1)
    c_lo = jnp.bitwise_and(c, _POS_LOW - 1).astype(F32)
    c_hi = jnp.where(c >= _POS_LOW, float(_POS_LOW), 0.0)
    kext = jnp.where(lane_k < 3, -piece, jnp.where(lane_k < 6, -float(_POS_LOW) * piece,
                     jnp.where(lane_k < 9, c_lo, jnp.where(lane_k < _ALIBI_COLS, c_hi, 0.0))))
    kx_scr[...] = kext.astype(BF16)
    vx_scr[...] = jnp.where(lane_k == 0, 1.0, 0.0).astype(BF16)


def _diff_kernel(q_ref, k_ref, v_ref, lq1_ref, lk1_ref, lq2_ref, lk2_ref, g_ref, slope_ref,
                 o_ref, lhs_scr, kx_scr, vx_scr, m_scr, acc_scr, s_scr, k2max_scr, *, tq,
                 lambda_init):
    h = pl.program_id(0)
    i = pl.program_id(1)
    slope = slope_ref[h]

    @pl.when(i == 0)
    def _head_init():
        _diff_head_tables(slope, lhs_scr, kx_scr, vx_scr, tq)
        k2 = jnp.square(k_ref[...].astype(F32))
        k2max = jnp.maximum(*[jnp.max(jnp.sum(k2 * msk, axis=-1, keepdims=True), axis=0,
                                      keepdims=True) for msk in _half_masks(F32)])
        k2max_scr[...] = jnp.broadcast_to(k2max, k2max_scr.shape)

    mlo, mhi = _half_masks(BF16)
    q = q_ref[...] * (B_HEAD_DIM ** -0.5)
    lhs_scr[0:tq, 0:LANES] = q * mlo
    lhs_scr[tq:2 * tq, 0:LANES] = q * mhi
    m_scr[...] = jnp.full(m_scr.shape, -jnp.inf, F32)
    acc_scr[...] = jnp.zeros(acc_scr.shape, F32)

    def scores(j, slot):
        start = pl.multiple_of(j * tq, tq)
        kx = jnp.concatenate([k_ref[pl.ds(start, tq), :], kx_scr[...]], axis=1)
        s_scr[slot] = lax.dot_general(lhs_scr[...], kx, (((1,), (1,)), ((), ())),
                                      preferred_element_type=F32)

    def accumulate(j, slot, diagonal):
        start = pl.multiple_of(j * tq, tq)
        vx = jnp.concatenate([v_ref[pl.ds(start, tq), :], vx_scr[...]], axis=1)
        for half in range(2):
            rows = slice(half * tq, (half + 1) * tq)
            s = s_scr[slot, rows, :]
            if diagonal:
                row = lax.broadcasted_iota(jnp.int32, (tq, tq), 0)
                col = lax.broadcasted_iota(jnp.int32, (tq, tq), 1)
                s = jnp.where(row >= col, s, NEG)
                off = 0.0
            else:
                off = slope * ((i - j) * tq).astype(F32)
            m_old = m_scr[rows, :]
            m_new = jnp.maximum(m_old, jnp.max(s, axis=-1, keepdims=True) - off)
            p = jnp.exp(s - (m_new + off))
            alpha = jnp.exp(m_old - m_new)
            acc_scr[rows, :] = alpha * acc_scr[rows, :] + jnp.dot(
                p.astype(BF16), vx, preferred_element_type=F32)
            m_scr[rows, :] = m_new

    scores(i, 0)
    scores(jnp.maximum(i - 1, 0), 1)
    accumulate(i, 0, True)
    qf = (q * mlo).astype(F32), (q * mhi).astype(F32)
    q2max = jnp.maximum(*[jnp.max(jnp.sum(c * c, axis=-1, keepdims=True), axis=0, keepdims=True)
                          for c in qf])
    m_min = jnp.min(m_scr[...], axis=0, keepdims=True)
    reach = (jnp.sqrt(q2max * k2max_scr[0:1, 0:1]) * _BOUND_SLACK + slope * (tq - 1)
             - m_min + _EXP_ZERO_BELOW + _BOUND_MARGIN) / (slope * tq)
    n_far = jnp.minimum(reach, i.astype(F32)).astype(jnp.int32)[0, 0]

    def pair(jj, carry):
        j = i - 1 - 2 * jj
        scores(j - 1, 0)
        accumulate(j, 1, False)
        scores(jnp.maximum(j - 2, 0), 1)
        accumulate(j - 1, 0, False)
        return carry

    lax.fori_loop(0, n_far // 2, pair, 0)

    @pl.when(n_far % 2 == 1)
    def _odd_tail():
        accumulate(i - n_far, 1, False)

    lam = (jnp.exp(jnp.sum(lq1_ref[...] * lk1_ref[...], axis=-1, keepdims=True))
           - jnp.exp(jnp.sum(lq2_ref[...] * lk2_ref[...], axis=-1, keepdims=True))
           + lambda_init)
    o0 = acc_scr[0:tq, 0:B_V_DIM] / acc_scr[0:tq, B_V_DIM:B_V_DIM + 1]
    o1 = acc_scr[tq:2 * tq, 0:B_V_DIM] / acc_scr[tq:2 * tq, B_V_DIM:B_V_DIM + 1]
    o = o0 - lam * o1
    o = o * lax.rsqrt(jnp.mean(o * o, axis=-1, keepdims=True) + LN_EPS)
    o = o * g_ref[...] * (1.0 - lambda_init)
    o_ref[...] = o.astype(o_ref.dtype)


def _diff_attention(qkv, lq1, lk1, lq2, lk2, subln_g, lambda_init, tq=512):
    s = qkv.shape[0]
    tq = _tile(s, tq)
    assert tq <= 2 * _POS_LOW and tq % _POS_LOW == 0, tq
    vec = lambda n: pl.BlockSpec((1, n), lambda h, i: (0, 0))
    kern = functools.partial(_diff_kernel, tq=tq, lambda_init=lambda_init)
    return pl.pallas_call(
        kern,
        grid=(B_HEADS, s // tq),
        in_specs=[pl.BlockSpec((tq, B_V_DIM), lambda h, i: (i, h)),
                  pl.BlockSpec((s, B_V_DIM), lambda h, i: (0, B_K_BLOCK0 + h)),
                  pl.BlockSpec((s, B_V_DIM), lambda h, i: (0, B_V_BLOCK0 + h)),
                  vec(B_HEAD_DIM), vec(B_HEAD_DIM), vec(B_HEAD_DIM), vec(B_HEAD_DIM),
                  vec(B_V_DIM), pl.BlockSpec(memory_space=pltpu.SMEM)],
        out_specs=pl.BlockSpec((tq, B_V_DIM), lambda h, i: (i, h)),
        out_shape=jax.ShapeDtypeStruct((s, B_HEADS * B_V_DIM), BF16),
        scratch_shapes=[pltpu.VMEM((2 * tq, 2 * LANES), BF16), pltpu.VMEM((tq, LANES), BF16),
                        pltpu.VMEM((tq, LANES), BF16), pltpu.VMEM((2 * tq, 1), F32),
                        pltpu.VMEM((2 * tq, 2 * LANES), F32), pltpu.VMEM((2, 2 * tq, tq), F32),
                        pltpu.VMEM((8, LANES), F32)],
        compiler_params=_params("arbitrary", "arbitrary"),
        name="diff_attention",
    )(qkv, qkv, qkv, lq1.reshape(1, -1).astype(F32), lk1.reshape(1, -1).astype(F32),
      lq2.reshape(1, -1).astype(F32), lk2.reshape(1, -1).astype(F32),
      subln_g.reshape(1, -1).astype(F32), _alibi_slopes(B_HEADS))


_N_RANKED = P_TOPK + 1
_PAIR_RANKS = [(i, j) for i in range(_N_RANKED) for j in range(_N_RANKED)
               if (i + 1) * (j + 1) <= _N_RANKED]
SUBLANES = 8
_CAND_ROWS = 64
assert len(_PAIR_RANKS) <= _CAND_ROWS


def _sorting_network(n):
    pairs = []
    p = 1
    while p < n:
        k = p
        while k >= 1:
            for j in range(k % p, n - k, 2 * k):
                for i in range(min(k, n - j - k)):
                    if (i + j) // (2 * p) == (i + j + k) // (2 * p):
                        pairs.append((i + j, i + j + k))
            k //= 2
        p *= 2
    return pairs


def _top_values(work, count):
    n = work.shape[0] // SUBLANES
    r = [work[SUBLANES * k:SUBLANES * (k + 1), :] for k in range(n)]
    for lo, hi in _sorting_network(n):
        r[lo], r[hi] = jnp.maximum(r[lo], r[hi]), jnp.minimum(r[lo], r[hi])
    out = []
    for k in range(count):
        top = jnp.max(r[0], axis=0, keepdims=True)
        out.append(top)
        needed = count - 1 - k
        hit = r[0] == top
        for i in range(min(needed, n - 1)):
            r[i] = jnp.where(hit, r[i + 1], r[i])
        if needed >= n:
            r[n - 1] = jnp.where(hit, -jnp.inf, r[n - 1])
    return out


def _peer_score_kernel(q_ref, sk_ref, thr_ref, e1_ref, a2_ref, e2_ref, cand_scr):
    tt = q_ref.shape[0]
    cand_scr[...] = jnp.full(cand_scr.shape, -jnp.inf, F32)
    for h in range(P_HEADS):
        st = []
        for c in range(2):
            qs = q_ref[:, (2 * h + c) * P_HALF:(2 * h + c + 1) * P_HALF]
            st.append(lax.dot_general(sk_ref[h, c], qs, (((1,), (1,)), ((), ())),
                                      preferred_element_type=F32))
        a = _top_values(st[0], _N_RANKED)
        b = _top_values(st[1], _N_RANKED)
        for r, (i, j) in enumerate(_PAIR_RANKS):
            cand_scr[r:r + 1, :] = a[i] + b[j]
        best = _top_values(cand_scr[...], _N_RANKED)
        z = jnp.zeros((1, tt), F32)
        for v in best[:P_TOPK]:
            z = z + jnp.exp(v - best[0])
        cut = 0.5 * (best[P_TOPK - 1] + best[P_TOPK])
        thr_ref[h] = cut - st[0]
        a2_ref[h] = st[1]
        e1_ref[h] = jnp.exp(st[0] - a[0]) / z
        e2_ref[h] = jnp.exp(st[1] - b[0])


def _peer_scores(q, subkeys, tt=512):
    s = q.shape[0]
    tt = _tile(s, tt)
    big = jax.ShapeDtypeStruct((P_HEADS, N_KEYS, s), F32)
    big_spec = pl.BlockSpec((P_HEADS, N_KEYS, tt), lambda t: (0, 0, t))
    return pl.pallas_call(
        _peer_score_kernel,
        grid=(s // tt,),
        in_specs=[pl.BlockSpec((tt, q.shape[1]), lambda t: (t, 0)),
                  pl.BlockSpec(subkeys.shape, lambda t: (0, 0, 0, 0))],
        out_specs=[big_spec, big_spec, big_spec, big_spec],
        out_shape=[big, big, big, big],
        scratch_shapes=[pltpu.VMEM((_CAND_ROWS, tt), F32)],
        compiler_params=_params("parallel"),
        name="peer_scores",
    )(q, subkeys)


def _gelu(a):
    return 0.5 * a * (1.0 + lax.erf(a * np.float32(np.sqrt(0.5))))


def _peer_dense_kernel(x_ref, u_ref, v_ref, thr_ref, e1_ref, a2_ref, e2_ref, xres_ref, g_ref,
                       b_ref, of_ref, ob_ref, act_scr, h_scr, acc_scr, *, n_etiles):
    s = pl.program_id(0)
    _, te, tt = act_scr.shape
    e_prev = lax.rem(jnp.maximum(s - 1, 0), n_etiles)

    @pl.when(s == 0)
    def _first():
        act_scr[1] = jnp.zeros((te, tt), F32)

    @pl.when(e_prev == 0)
    def _init():
        acc_scr[...] = jnp.zeros(acc_scr.shape, F32)

    def body(write_slot, read_slot):
        act_scr[write_slot] = lax.dot_general(u_ref[...], x_ref[...], (((1,), (1,)), ((), ())),
                                              preferred_element_type=F32)
        for il in range(te // N_KEYS):
            rows = slice(il * N_KEYS, (il + 1) * N_KEYS)
            for lg in range(tt // LANES):
                cols = slice(lg * LANES, (lg + 1) * LANES)
                gate = jnp.zeros((N_KEYS, LANES), F32)
                for h in range(P_HEADS):
                    w = e2_ref[h, :, cols] * e1_ref[h, il:il + 1, cols]
                    gate = jnp.where(a2_ref[h, :, cols] > thr_ref[h, il:il + 1, cols],
                                     gate + w, gate)
                h_scr[rows, cols] = (gate * _gelu(act_scr[read_slot, rows, cols])).astype(BF16)
        acc_scr[...] += lax.dot_general(h_scr[...], v_ref[...], (((0,), (0,)), ((), ())),
                                        preferred_element_type=F32)

    @pl.when(lax.rem(s, 2) == 0)
    def _even():
        body(0, 1)

    @pl.when(lax.rem(s, 2) == 1)
    def _odd():
        body(1, 0)

    @pl.when((e_prev == n_etiles - 1) & (s > 0))
    def _finish():
        out = _layer_norm_rows(ALPHA * xres_ref[...] + acc_scr[...], g_ref[...], b_ref[...])
        of_ref[...] = out
        ob_ref[...] = out.astype(BF16)


def _peer_dense_ln(x_bf, u_bf, v_bf, thr, e1, a2, e2, x_res, g, b, tt=512, te=1024):
    s, d = x_bf.shape
    n_exp = u_bf.shape[0]
    tt, te = _tile(s, tt), _tile(n_exp, te)
    ne = n_exp // te
    n_items = (s // tt) * ne
    cur = lambda i: jnp.minimum(i, n_items - 1)
    prev = lambda i: jnp.maximum(i - 1, 0)
    rows_spec = pl.BlockSpec((P_HEADS, te // N_KEYS, tt),
                             lambda i: (0, prev(i) % ne, prev(i) // ne))
    once = pl.Buffered(1)
    full_spec = pl.BlockSpec((P_HEADS, N_KEYS, tt), lambda i: (0, 0, prev(i) // ne),
                             pipeline_mode=once)
    tok_spec = pl.BlockSpec((tt, d), lambda i: (prev(i) // ne, 0))
    res_spec = pl.BlockSpec((tt, d), lambda i: (prev(i) // ne, 0), pipeline_mode=once)
    vec_spec = pl.BlockSpec((1, d), lambda i: (0, 0))
    return pl.pallas_call(
        functools.partial(_peer_dense_kernel, n_etiles=ne),
        grid=(n_items + 1,),
        in_specs=[pl.BlockSpec((tt, d), lambda i: (cur(i) // ne, 0)),
                  pl.BlockSpec((te, d), lambda i: (cur(i) % ne, 0)),
                  pl.BlockSpec((te, d), lambda i: (prev(i) % ne, 0)),
                  rows_spec, rows_spec, full_spec, full_spec, res_spec, vec_spec, vec_spec],
        out_specs=[tok_spec, tok_spec],
        out_shape=[jax.ShapeDtypeStruct((s, d), F32), jax.ShapeDtypeStruct((s, d), BF16)],
        scratch_shapes=[pltpu.VMEM((2, te, tt), F32), pltpu.VMEM((te, tt), BF16),
                        pltpu.VMEM((tt, d), F32)],
        compiler_params=_params("arbitrary"),
        name="peer_dense",
    )(x_bf, u_bf, v_bf, thr, e1, a2, e2, x_res, g.reshape(1, d), b.reshape(1, d))


def _peer_ffn_ln(x_f, x_bf, w_q, subkeys, u, v, g, b):
    q = _matmul(x_bf, w_q.astype(BF16), BF16)
    thr, e1, a2, e2 = _peer_scores(q, subkeys.astype(BF16))
    return _peer_dense_ln(x_bf, u.astype(BF16), v.astype(BF16), thr, e1, a2, e2, x_f, g, b)


def kernel(x, a_w_qkv, a_sinks, a_w_o, b_w_qkv, b_lambda_q1, b_lambda_k1, b_lambda_q2,
           b_lambda_k2, b_subln_g, b_w_o, ln1_g, ln1_b, ln2_g, ln2_b,
           peer_w_q, peer_subkeys, peer_u, peer_v):
    bsz, seq, d = x.shape
    xf = x.reshape(bsz * seq, d).astype(F32)
    assert bsz == 1, "attention kernels index one sequence"
    xb = xf.astype(BF16)
    for i in range(DEPTH):
        j = i // 2
        if i % 2 == 0:
            qkv = _matmul(xb, _swa_weight(a_w_qkv[j]), BF16)
            mix = _swa_attention(qkv, a_sinks[j])
            w_o = a_w_o[j]
        else:
            lambda_init = 0.8 - 0.6 * float(np.exp(-0.3 * i))
            qkv = _matmul(xb, b_w_qkv[j].astype(BF16), BF16)
            mix = _diff_attention(qkv, b_lambda_q1[j], b_lambda_k1[j], b_lambda_q2[j],
                                  b_lambda_k2[j], b_subln_g[j], lambda_init)
            w_o = b_w_o[j]
        xf, xb = _proj_residual_ln(mix, w_o.astype(BF16), xf, ln1_g[i], ln1_b[i])
        xf, xb = _peer_ffn_ln(xf, xb, peer_w_q[i], peer_subkeys[i], peer_u[i], peer_v[i],
                              ln2_g[i], ln2_b[i])
    return xf.reshape(bsz, seq, d).astype(x.dtype)
```

```python
import functools

import jax
import jax.numpy as jnp
import numpy as np
from jax import lax
from jax.experimental import pallas as pl
from jax.experimental.pallas import tpu as pltpu

F32 = jnp.float32
BF16 = jnp.bfloat16

DEPTH = 4
A_HEADS, A_KV_HEADS, A_HEAD_DIM, WINDOW = 32, 4, 64, 128
A_GROUP = A_HEADS // A_KV_HEADS
B_HEADS, B_HEAD_DIM = 16, 64
B_V_DIM = 2 * B_HEAD_DIM
P_HEADS, N_KEYS, P_TOPK, P_HALF = 8, 128, 16, 128
LN_EPS = 1e-5
NEG = -1e30
ALPHA = (2.0 * DEPTH) ** 0.25

LANES = 128
V7X_VMEM_LIMIT_BYTES = 56 * 1024 * 1024


def _tile(n, pref):
    t = min(n, pref)
    assert n % t == 0, (n, t)
    return t


def _params(*sem, flags=None):
    return pltpu.CompilerParams(dimension_semantics=sem, vmem_limit_bytes=V7X_VMEM_LIMIT_BYTES,
                                flags=flags)


def _alibi_slopes(n):
    return jnp.asarray((2.0 ** (-8.0 * np.arange(1, n + 1) / n)).astype(np.float32))


def _mm_kernel(a_ref, b_ref, o_ref):
    o_ref[...] = jnp.dot(a_ref[...], b_ref[...], preferred_element_type=F32).astype(o_ref.dtype)


def _matmul(a, b, layer, out_dtype, tm=1024, tn=512):
    m, k = a.shape
    n = b.shape[2]
    tm, tn = _tile(m, tm), _tile(n, tn)
    return pl.pallas_call(
        _mm_kernel,
        grid=(m // tm, n // tn),
        in_specs=[pl.BlockSpec((tm, k), lambda i, j: (i, 0)),
                  pl.BlockSpec((None, k, tn), lambda i, j: (layer, 0, j))],
        out_specs=pl.BlockSpec((tm, tn), lambda i, j: (i, j)),
        out_shape=jax.ShapeDtypeStruct((m, n), out_dtype),
        compiler_params=_params("parallel", "parallel"),
        name="matmul",
    )(a, b)


def _layer_norm_rows(z, g, b):
    mu = jnp.mean(z, axis=-1, keepdims=True)
    zc = z - mu
    var = jnp.mean(zc * zc, axis=-1, keepdims=True)
    return zc * lax.rsqrt(var + LN_EPS) * g + b


def _proj_ln_kernel(a_ref, w_ref, x_ref, g_ref, b_ref, of_ref, ob_ref):
    y = jnp.dot(a_ref[...], w_ref[...], preferred_element_type=F32)
    out = _layer_norm_rows(ALPHA * x_ref[...] + y, g_ref[...], b_ref[...])
    of_ref[...] = out
    ob_ref[...] = out.astype(BF16)


def _proj_residual_ln(a, w, layer, x, g, b, tm=256):
    s, k = a.shape
    d = w.shape[2]
    tm = _tile(s, tm)
    row = lambda i: (i, 0)
    fixed = lambda i: (0, 0)
    return pl.pallas_call(
        _proj_ln_kernel,
        grid=(s // tm,),
        in_specs=[pl.BlockSpec((tm, k), row), pl.BlockSpec((None, k, d), lambda i: (layer, 0, 0)),
                  pl.BlockSpec((tm, d), row), pl.BlockSpec((1, d), fixed),
                  pl.BlockSpec((1, d), fixed)],
        out_specs=[pl.BlockSpec((tm, d), row), pl.BlockSpec((tm, d), row)],
        out_shape=[jax.ShapeDtypeStruct((s, d), F32), jax.ShapeDtypeStruct((s, d), BF16)],
        compiler_params=_params("parallel"),
        name="proj_residual_ln",
    )(a, w, x, g.reshape(1, d), b.reshape(1, d))


A_Q_COLS = A_HEADS * A_HEAD_DIM
A_PAIR_COLS = 2 * A_HEAD_DIM


def _swa_weight(w_qkv):
    n, d = w_qkv.shape[:2]
    wq = w_qkv[..., :A_Q_COLS]
    kv = (n, d, A_KV_HEADS, A_HEAD_DIM)
    wk = w_qkv[..., A_Q_COLS:A_Q_COLS + A_KV_HEADS * A_HEAD_DIM].reshape(kv)
    wv = w_qkv[..., A_Q_COLS + A_KV_HEADS * A_HEAD_DIM:].reshape(kv)
    dup = lambda w: jnp.concatenate([w, w], axis=-1).reshape(n, d, A_KV_HEADS * A_PAIR_COLS)
    return jnp.concatenate([wq, dup(wk), dup(wv)], axis=-1).astype(BF16)


def _half_masks(dtype):
    lane = lax.broadcasted_iota(jnp.int32, (1, LANES), 1)
    lo = (lane < LANES // 2).astype(F32)
    return lo.astype(dtype), (1.0 - lo).astype(dtype)


def _swa_kernel(q_ref, kp_ref, kc_ref, vp_ref, vc_ref, sink_ref, slope_ref, o_ref):
    i = pl.program_id(0)
    blk = WINDOW
    row = lax.broadcasted_iota(jnp.int32, (2 * blk, 2 * blk), 0)
    col = lax.broadcasted_iota(jnp.int32, (2 * blk, 2 * blk), 1)
    dist = jnp.bitwise_and(row, blk - 1) + blk - col
    valid = (dist >= 0) & (dist < WINDOW) & ((col >= blk) | (i > 0))
    distf = dist.astype(F32)
    top = row < blk
    top_col = top[:, :1]
    mlo, mhi = _half_masks(BF16)
    lo_f = lax.broadcasted_iota(jnp.int32, (blk, LANES), 1) < LANES // 2
    for g in range(A_KV_HEADS):
        kv = slice(g * A_PAIR_COLS, (g + 1) * A_PAIR_COLS)
        kband = jnp.concatenate([kp_ref[:, kv], kc_ref[:, kv]], axis=0)
        vband = jnp.concatenate([vp_ref[:, kv], vc_ref[:, kv]], axis=0)
        for p in range(A_GROUP // 2):
            h0 = g * A_GROUP + 2 * p
            qcols = slice((h0 // 2) * LANES, (h0 // 2 + 1) * LANES)
            q2 = q_ref[:, qcols]
            lhs = jnp.concatenate([q2 * mlo, q2 * mhi], axis=0)
            s = lax.dot_general(lhs, kband, (((1,), (1,)), ((), ())),
                                preferred_element_type=F32) * (A_HEAD_DIM ** -0.5)
            slope = jnp.where(top, slope_ref[h0], slope_ref[h0 + 1])
            sink = jnp.where(top_col, sink_ref[h0], sink_ref[h0 + 1])
            s = s - slope * distf
            s = jnp.where(valid, s, NEG)
            m = jnp.maximum(jnp.max(s, axis=-1, keepdims=True), sink)
            pexp = jnp.exp(s - m)
            denom = jnp.sum(pexp, axis=-1, keepdims=True) + jnp.exp(sink - m)
            probs = (pexp / denom).astype(BF16)
            o2 = jnp.dot(probs, vband, preferred_element_type=F32)
            o_ref[:, qcols] = jnp.where(lo_f, o2[:blk], o2[blk:]).astype(o_ref.dtype)


def _swa_attention(qkv, sinks):
    s = qkv.shape[0]
    nb = s // WINDOW
    prev = lambda i: jnp.maximum(i - 1, 0)
    smem = pl.BlockSpec(memory_space=pltpu.SMEM)
    kv_cols = A_KV_HEADS * A_PAIR_COLS
    k_blk, v_blk = A_Q_COLS // kv_cols, A_Q_COLS // kv_cols + 1
    return pl.pallas_call(
        _swa_kernel,
        grid=(nb,),
        in_specs=[pl.BlockSpec((WINDOW, A_Q_COLS), lambda i: (i, 0)),
                  pl.BlockSpec((WINDOW, kv_cols), lambda i: (prev(i), k_blk)),
                  pl.BlockSpec((WINDOW, kv_cols), lambda i: (i, k_blk)),
                  pl.BlockSpec((WINDOW, kv_cols), lambda i: (prev(i), v_blk)),
                  pl.BlockSpec((WINDOW, kv_cols), lambda i: (i, v_blk)),
                  smem, smem],
        out_specs=pl.BlockSpec((WINDOW, A_Q_COLS), lambda i: (i, 0)),
        out_shape=jax.ShapeDtypeStruct((s, A_Q_COLS), BF16),
        compiler_params=_params("parallel"),
        name="swa_attention",
    )(qkv, qkv, qkv, qkv, qkv, sinks.astype(F32), _alibi_slopes(A_HEADS))


B_QK_COLS = B_HEADS * 2 * B_HEAD_DIM
B_K_BLOCK0 = B_QK_COLS // B_V_DIM
B_V_BLOCK0 = 2 * B_K_BLOCK0


_ALIBI_COLS = 12
_POS_LOW = 256
_EXP_ZERO_BELOW = 110.0
_BOUND_SLACK = 1.001
_BOUND_MARGIN = 1.0


def _diff_head_tables(slope, lhs_scr, kx_scr, vx_scr, tq):
    sv = jnp.full((1, LANES), slope, F32)
    s1 = sv.astype(BF16).astype(F32)
    s2 = (sv - s1).astype(BF16).astype(F32)
    s3 = (sv - s1 - s2).astype(BF16).astype(F32)
    lane = lax.broadcasted_iota(jnp.int32, (1, LANES), 1)
    third = lambda k: (lane == k) | (lane == k + 3) | (lane == k + 6) | (lane == k + 9)
    piece = jnp.where(third(0), s1, jnp.where(third(1), s2, s3))

    row = lax.broadcasted_iota(jnp.int32, (2 * tq, LANES), 0)
    lane_q = lax.broadcasted_iota(jnp.int32, (2 * tq, LANES), 1)
    r = jnp.where(row >= tq, row - tq, row)
    r_lo = jnp.bitwise_and(r, _POS_LOW - 1).astype(F32)
    r_hi = jnp.where(r >= _POS_LOW, 1.0, 0.0)
    ext = jnp.where(lane_q < 3, r_lo, jnp.where(lane_q < 6, r_hi,
                    jnp.where(lane_q < _ALIBI_COLS, piece, 0.0)))
    lhs_scr[:, LANES:2 * LANES] = ext.astype(BF16)

    c = lax.broadcasted_iota(jnp.int32, (tq, LANES), 0)
    lane_k = lax.broadcasted_iota(jnp.int32, (tq, LANES), 1)
    c_lo = jnp.bitwise_and(c, _POS_LOW - 1).astype(F32)
    c_hi = jnp.where(c >= _POS_LOW, float(_POS_LOW), 0.0)
    kext = jnp.where(lane_k < 3, -piece, jnp.where(lane_k < 6, -float(_POS_LOW) * piece,
                     jnp.where(lane_k < 9, c_lo, jnp.where(lane_k < _ALIBI_COLS, c_hi, 0.0))))
    kx_scr[...] = kext.astype(BF16)
    vx_scr[...] = jnp.where(lane_k == 0, 1.0, 0.0).astype(BF16)


def _diff_kernel(q_ref, k_ref, v_ref, lq1_ref, lk1_ref, lq2_ref, lk2_ref, g_ref, slope_ref,
                 o_ref, lhs_scr, kx_scr, vx_scr, m_scr, acc_scr, s_scr, k2max_scr, *, tq,
                 lambda_init):
    h = pl.program_id(0)
    i = pl.program_id(1)
    slope = slope_ref[h]

    @pl.when(i == 0)
    def _head_init():
        _diff_head_tables(slope, lhs_scr, kx_scr, vx_scr, tq)
        k2 = jnp.square(k_ref[...].astype(F32))
        k2max = jnp.maximum(*[jnp.max(jnp.sum(k2 * msk, axis=-1, keepdims=True), axis=0,
                                      keepdims=True) for msk in _half_masks(F32)])
        k2max_scr[...] = jnp.broadcast_to(k2max, k2max_scr.shape)

    mlo, mhi = _half_masks(BF16)
    q = q_ref[...] * (B_HEAD_DIM ** -0.5)
    lhs_scr[0:tq, 0:LANES] = q * mlo
    lhs_scr[tq:2 * tq, 0:LANES] = q * mhi
    m_scr[...] = jnp.full(m_scr.shape, -jnp.inf, F32)
    acc_scr[...] = jnp.zeros(acc_scr.shape, F32)

    def scores(j, slot):
        start = pl.multiple_of(j * tq, tq)
        kx = jnp.concatenate([k_ref[pl.ds(start, tq), :], kx_scr[...]], axis=1)
        s_scr[slot] = lax.dot_general(lhs_scr[...], kx, (((1,), (1,)), ((), ())),
                                      preferred_element_type=F32)

    def accumulate(j, slot, diagonal):
        start = pl.multiple_of(j * tq, tq)
        vx = jnp.concatenate([v_ref[pl.ds(start, tq), :], vx_scr[...]], axis=1)
        for half in range(2):
            rows = slice(half * tq, (half + 1) * tq)
            s = s_scr[slot, rows, :]
            if diagonal:
                row = lax.broadcasted_iota(jnp.int32, (tq, tq), 0)
                col = lax.broadcasted_iota(jnp.int32, (tq, tq), 1)
                s = jnp.where(row >= col, s, NEG)
                off = 0.0
            else:
                off = slope * ((i - j) * tq).astype(F32)
            m_old = m_scr[rows, :]
            m_new = jnp.maximum(m_old, jnp.max(s, axis=-1, keepdims=True) - off)
            p = jnp.exp(s - (m_new + off))
            alpha = jnp.exp(m_old - m_new)
            acc_scr[rows, :] = alpha * acc_scr[rows, :] + jnp.dot(
                p.astype(BF16), vx, preferred_element_type=F32)
            m_scr[rows, :] = m_new

    scores(i, 0)
    scores(jnp.maximum(i - 1, 0), 1)
    accumulate(i, 0, True)
    qf = (q * mlo).astype(F32), (q * mhi).astype(F32)
    q2max = jnp.maximum(*[jnp.max(jnp.sum(c * c, axis=-1, keepdims=True), axis=0, keepdims=True)
                          for c in qf])
    m_min = jnp.min(m_scr[...], axis=0, keepdims=True)
    reach = (jnp.sqrt(q2max * k2max_scr[0:1, 0:1]) * _BOUND_SLACK + slope * (tq - 1)
             - m_min + _EXP_ZERO_BELOW + _BOUND_MARGIN) / (slope * tq)
    n_far = jnp.floor(jnp.minimum(reach, i.astype(F32))).astype(jnp.int32)[0, 0]

    def pair(jj, carry):
        j = i - 1 - 2 * jj
        scores(j - 1, 0)
        accumulate(j, 1, False)
        scores(jnp.maximum(j - 2, 0), 1)
        accumulate(j - 1, 0, False)
        return carry

    lax.fori_loop(0, jnp.right_shift(n_far, 1), pair, 0)

    @pl.when(jnp.bitwise_and(n_far, 1) == 1)
    def _odd_tail():
        accumulate(i - n_far, 1, False)

    lam = (jnp.exp(jnp.sum(lq1_ref[...] * lk1_ref[...], axis=-1, keepdims=True))
           - jnp.exp(jnp.sum(lq2_ref[...] * lk2_ref[...], axis=-1, keepdims=True))
           + lambda_init)
    o0 = acc_scr[0:tq, 0:B_V_DIM] / acc_scr[0:tq, B_V_DIM:B_V_DIM + 1]
    o1 = acc_scr[tq:2 * tq, 0:B_V_DIM] / acc_scr[tq:2 * tq, B_V_DIM:B_V_DIM + 1]
    o = o0 - lam * o1
    o = o * lax.rsqrt(jnp.mean(o * o, axis=-1, keepdims=True) + LN_EPS)
    o = o * g_ref[...] * (1.0 - lambda_init)
    o_ref[...] = o.astype(o_ref.dtype)


def _diff_attention(qkv, lq1, lk1, lq2, lk2, subln_g, lambda_init, tq=512):
    s = qkv.shape[0]
    tq = _tile(s, tq)
    assert tq <= 2 * _POS_LOW and tq % _POS_LOW == 0, tq
    vec = lambda n: pl.BlockSpec((1, n), lambda h, i: (0, 0))
    kern = functools.partial(_diff_kernel, tq=tq, lambda_init=lambda_init)
    return pl.pallas_call(
        kern,
        grid=(B_HEADS, s // tq),
        in_specs=[pl.BlockSpec((tq, B_V_DIM), lambda h, i: (i, h)),
                  pl.BlockSpec((s, B_V_DIM), lambda h, i: (0, B_K_BLOCK0 + h)),
                  pl.BlockSpec((s, B_V_DIM), lambda h, i: (0, B_V_BLOCK0 + h)),
                  vec(B_HEAD_DIM), vec(B_HEAD_DIM), vec(B_HEAD_DIM), vec(B_HEAD_DIM),
                  vec(B_V_DIM), pl.BlockSpec(memory_space=pltpu.SMEM)],
        out_specs=pl.BlockSpec((tq, B_V_DIM), lambda h, i: (i, h)),
        out_shape=jax.ShapeDtypeStruct((s, B_HEADS * B_V_DIM), BF16),
        scratch_shapes=[pltpu.VMEM((2 * tq, 2 * LANES), BF16), pltpu.VMEM((tq, LANES), BF16),
                        pltpu.VMEM((tq, LANES), BF16), pltpu.VMEM((2 * tq, 1), F32),
                        pltpu.VMEM((2 * tq, 2 * LANES), F32), pltpu.VMEM((2, 2 * tq, tq), F32),
                        pltpu.VMEM((8, LANES), F32)],
        compiler_params=_params("arbitrary", "arbitrary"),
        name="diff_attention",
    )(qkv, qkv, qkv, lq1.reshape(1, -1).astype(F32), lk1.reshape(1, -1).astype(F32),
      lq2.reshape(1, -1).astype(F32), lk2.reshape(1, -1).astype(F32),
      subln_g.reshape(1, -1).astype(F32), _alibi_slopes(B_HEADS))


_N_RANKED = P_TOPK + 1
_PAIR_RANKS = [(i, j) for i in range(_N_RANKED) for j in range(_N_RANKED)
               if (i + 1) * (j + 1) <= _N_RANKED]
SUBLANES = 8
_CAND_ROWS = 64
assert len(_PAIR_RANKS) <= _CAND_ROWS


def _sorting_network(n):
    pairs = []
    p = 1
    while p < n:
        k = p
        while k >= 1:
            for j in range(k % p, n - k, 2 * k):
                for i in range(min(k, n - j - k)):
                    if (i + j) // (2 * p) == (i + j + k) // (2 * p):
                        pairs.append((i + j, i + j + k))
            k //= 2
        p *= 2
    return pairs


def _top_values(work, count):
    n = work.shape[0] // SUBLANES
    r = [work[SUBLANES * k:SUBLANES * (k + 1), :] for k in range(n)]
    for lo, hi in _sorting_network(n):
        r[lo], r[hi] = jnp.maximum(r[lo], r[hi]), jnp.minimum(r[lo], r[hi])
    out = []
    for k in range(count):
        top = jnp.max(r[0], axis=0, keepdims=True)
        out.append(top)
        needed = count - 1 - k
        hit = r[0] == top
        for i in range(min(needed, n - 1)):
            r[i] = jnp.where(hit, r[i + 1], r[i])
        if needed >= n:
            r[n - 1] = jnp.where(hit, -jnp.inf, r[n - 1])
    return out


def _peer_score_kernel(q_ref, sk_ref, thr_ref, e1_ref, a2_ref, e2_ref, cand_scr):
    tt = q_ref.shape[0]
    cand_scr[...] = jnp.full(cand_scr.shape, -jnp.inf, F32)
    for h in range(P_HEADS):
        st = []
        for c in range(2):
            qs = q_ref[:, (2 * h + c) * P_HALF:(2 * h + c + 1) * P_HALF]
            st.append(lax.dot_general(sk_ref[h, c], qs, (((1,), (1,)), ((), ())),
                                      preferred_element_type=F32))
        a = _top_values(st[0], _N_RANKED)
        b = _top_values(st[1], _N_RANKED)
        for r, (i, j) in enumerate(_PAIR_RANKS):
            cand_scr[r:r + 1, :] = a[i] + b[j]
        best = _top_values(cand_scr[...], _N_RANKED)
        z = jnp.zeros((1, tt), F32)
        for v in best[:P_TOPK]:
            z = z + jnp.exp(v - best[0])
        cut = 0.5 * (best[P_TOPK - 1] + best[P_TOPK])
        thr_ref[h] = cut - st[0]
        a2_ref[h] = st[1]
        e1_ref[h] = jnp.exp(st[0] - a[0]) / z
        e2_ref[h] = jnp.exp(st[1] - b[0])


def _peer_scores(q, subkeys, layer, tt=512):
    s = q.shape[0]
    tt = _tile(s, tt)
    big = jax.ShapeDtypeStruct((P_HEADS, N_KEYS, s), F32)
    big_spec = pl.BlockSpec((P_HEADS, N_KEYS, tt), lambda t: (0, 0, t))
    return pl.pallas_call(
        _peer_score_kernel,
        grid=(s // tt,),
        in_specs=[pl.BlockSpec((tt, q.shape[1]), lambda t: (t, 0)),
                  pl.BlockSpec((None,) + subkeys.shape[1:], lambda t: (layer, 0, 0, 0, 0))],
        out_specs=[big_spec, big_spec, big_spec, big_spec],
        out_shape=[big, big, big, big],
        scratch_shapes=[pltpu.VMEM((_CAND_ROWS, tt), F32)],
        compiler_params=_params("parallel"),
        name="peer_scores",
    )(q, subkeys)


def _gelu(a):
    return 0.5 * a * (1.0 + lax.erf(a * np.float32(np.sqrt(0.5))))


def _peer_dense_kernel(x_ref, u_ref, v_ref, thr_ref, e1_ref, a2_ref, e2_ref, xres_ref, g_ref,
                       b_ref, of_ref, ob_ref, act_scr, h_scr, acc_scr, *, n_etiles):
    s = pl.program_id(0)
    _, te, tt = act_scr.shape
    e_prev = jnp.maximum(s - 1, 0) % n_etiles

    @pl.when(s == 0)
    def _first():
        act_scr[1] = jnp.zeros((te, tt), F32)

    @pl.when(e_prev == 0)
    def _init():
        acc_scr[...] = jnp.zeros(acc_scr.shape, F32)

    def body(write_slot, read_slot):
        act_scr[write_slot] = lax.dot_general(u_ref[...], x_ref[...], (((1,), (1,)), ((), ())),
                                              preferred_element_type=F32)
        for il in range(te // N_KEYS):
            rows = slice(il * N_KEYS, (il + 1) * N_KEYS)
            for lg in range(tt // LANES):
                cols = slice(lg * LANES, (lg + 1) * LANES)
                gate = jnp.zeros((N_KEYS, LANES), F32)
                for h in range(P_HEADS):
                    w = e2_ref[h, :, cols] * e1_ref[h, il:il + 1, cols]
                    gate = jnp.where(a2_ref[h, :, cols] > thr_ref[h, il:il + 1, cols],
                                     gate + w, gate)
                h_scr[rows, cols] = (gate * _gelu(act_scr[read_slot, rows, cols])).astype(BF16)
        acc_scr[...] += lax.dot_general(h_scr[...], v_ref[...], (((0,), (0,)), ((), ())),
                                        preferred_element_type=F32)

    @pl.when(s % 2 == 0)
    def _even():
        body(0, 1)

    @pl.when(s % 2 == 1)
    def _odd():
        body(1, 0)

    @pl.when((e_prev == n_etiles - 1) & (s > 0))
    def _finish():
        out = _layer_norm_rows(ALPHA * xres_ref[...] + acc_scr[...], g_ref[...], b_ref[...])
        of_ref[...] = out
        ob_ref[...] = out.astype(BF16)


def _peer_dense_ln(x_bf, u_bf, v_bf, layer, thr, e1, a2, e2, x_res, g, b, tt=512, te=1024):
    s, d = x_bf.shape
    n_exp = u_bf.shape[1]
    tt, te = _tile(s, tt), _tile(n_exp, te)
    ne = n_exp // te
    n_items = (s // tt) * ne
    cur = lambda i: jnp.minimum(i, n_items - 1)
    prev = lambda i: jnp.maximum(i - 1, 0)
    rows_spec = pl.BlockSpec((P_HEADS, te // N_KEYS, tt),
                             lambda i: (0, prev(i) % ne, prev(i) // ne))
    once = pl.Buffered(1)
    full_spec = pl.BlockSpec((P_HEADS, N_KEYS, tt), lambda i: (0, 0, prev(i) // ne),
                             pipeline_mode=once)
    tok_spec = pl.BlockSpec((tt, d), lambda i: (prev(i) // ne, 0))
    res_spec = pl.BlockSpec((tt, d), lambda i: (prev(i) // ne, 0), pipeline_mode=once)
    vec_spec = pl.BlockSpec((1, d), lambda i: (0, 0))
    return pl.pallas_call(
        functools.partial(_peer_dense_kernel, n_etiles=ne),
        grid=(n_items + 1,),
        in_specs=[pl.BlockSpec((tt, d), lambda i: (cur(i) // ne, 0)),
                  pl.BlockSpec((None, te, d), lambda i: (layer, cur(i) % ne, 0)),
                  pl.BlockSpec((None, te, d), lambda i: (layer, prev(i) % ne, 0)),
                  rows_spec, rows_spec, full_spec, full_spec, res_spec, vec_spec, vec_spec],
        out_specs=[tok_spec, tok_spec],
        out_shape=[jax.ShapeDtypeStruct((s, d), F32), jax.ShapeDtypeStruct((s, d), BF16)],
        scratch_shapes=[pltpu.VMEM((2, te, tt), F32), pltpu.VMEM((te, tt), BF16),
                        pltpu.VMEM((tt, d), F32)],
        compiler_params=_params("arbitrary"),
        name="peer_dense",
    )(x_bf, u_bf, v_bf, thr, e1, a2, e2, x_res, g.reshape(1, d), b.reshape(1, d))


def _peer_ffn_ln(x_f, x_bf, wq_bf, subkeys_bf, u_bf, v_bf, layer, g, b):
    q = _matmul(x_bf, wq_bf, layer, BF16)
    thr, e1, a2, e2 = _peer_scores(q, subkeys_bf, layer)
    return _peer_dense_ln(x_bf, u_bf, v_bf, layer, thr, e1, a2, e2, x_f, g, b)


def kernel(x, a_w_qkv, a_sinks, a_w_o, b_w_qkv, b_lambda_q1, b_lambda_k1, b_lambda_q2,
           b_lambda_k2, b_subln_g, b_w_o, ln1_g, ln1_b, ln2_g, ln2_b,
           peer_w_q, peer_subkeys, peer_u, peer_v):
    bsz, seq, d = x.shape
    xf = x.reshape(bsz * seq, d).astype(F32)
    assert bsz == 1, "attention kernels index one sequence"
    xb = xf.astype(BF16)
    u_bf, v_bf = peer_u.astype(BF16), peer_v.astype(BF16)
    wq_bf, sk_bf = peer_w_q.astype(BF16), peer_subkeys.astype(BF16)
    a_qkv_bf, a_o_bf = _swa_weight(a_w_qkv), a_w_o.astype(BF16)
    b_qkv_bf, b_o_bf = b_w_qkv.astype(BF16), b_w_o.astype(BF16)
    for i in range(DEPTH):
        j = i // 2
        if i % 2 == 0:
            qkv = _matmul(xb, a_qkv_bf, j, BF16)
            mix = _swa_attention(qkv, a_sinks[j])
            w_o = a_o_bf
        else:
            lambda_init = 0.8 - 0.6 * float(np.exp(-0.3 * i))
            qkv = _matmul(xb, b_qkv_bf, j, BF16)
            mix = _diff_attention(qkv, b_lambda_q1[j], b_lambda_k1[j], b_lambda_q2[j],
                                  b_lambda_k2[j], b_subln_g[j], lambda_init)
            w_o = b_o_bf
        xf, xb = _proj_residual_ln(mix, w_o, j, xf, ln1_g[i], ln1_b[i])
        xf, xb = _peer_ffn_ln(xf, xb, wq_bf, sk_bf, u_bf, v_bf, i, ln2_g[i], ln2_b[i])
    return xf.reshape(bsz, seq, d).astype(x.dtype)
```

```python
import functools

import jax
import jax.numpy as jnp
import numpy as np
from jax import lax
from jax.experimental import pallas as pl
from jax.experimental.pallas import tpu as pltpu

F32 = jnp.float32
BF16 = jnp.bfloat16

DEPTH = 4
A_HEADS, A_KV_HEADS, A_HEAD_DIM, WINDOW = 32, 4, 64, 128
A_GROUP = A_HEADS // A_KV_HEADS
B_HEADS, B_HEAD_DIM = 16, 64
B_V_DIM = 2 * B_HEAD_DIM
P_HEADS, N_KEYS, P_TOPK, P_HALF = 8, 128, 16, 128
LN_EPS = 1e-5
NEG = -1e30
ALPHA = (2.0 * DEPTH) ** 0.25

LANES = 128
V7X_VMEM_LIMIT_BYTES = 56 * 1024 * 1024


def _tile(n, pref):
    t = min(n, pref)
    assert n % t == 0, (n, t)
    return t


def _params(*sem, flags=None):
    return pltpu.CompilerParams(dimension_semantics=sem, vmem_limit_bytes=V7X_VMEM_LIMIT_BYTES,
                                flags=flags)


def _alibi_slopes(n):
    return jnp.asarray((2.0 ** (-8.0 * np.arange(1, n + 1) / n)).astype(np.float32))


def _mm_kernel(a_ref, b_ref, o_ref):
    o_ref[...] = jnp.dot(a_ref[...], b_ref[...], preferred_element_type=F32).astype(o_ref.dtype)


def _matmul(a, b, layer, out_dtype, tm=1024, tn=1024):
    m, k = a.shape
    n = b.shape[2]
    tm, tn = _tile(m, tm), _tile(n, tn)
    return pl.pallas_call(
        _mm_kernel,
        grid=(m // tm, n // tn),
        in_specs=[pl.BlockSpec((tm, k), lambda i, j: (i, 0)),
                  pl.BlockSpec((None, k, tn), lambda i, j: (layer, 0, j))],
        out_specs=pl.BlockSpec((tm, tn), lambda i, j: (i, j)),
        out_shape=jax.ShapeDtypeStruct((m, n), out_dtype),
        compiler_params=_params("parallel", "parallel"),
        name="matmul",
    )(a, b)


def _layer_norm_rows(z, g, b):
    mu = jnp.mean(z, axis=-1, keepdims=True)
    zc = z - mu
    var = jnp.mean(zc * zc, axis=-1, keepdims=True)
    return zc * lax.rsqrt(var + LN_EPS) * g + b


def _proj_ln_kernel(a_ref, w_ref, x_ref, g_ref, b_ref, of_ref, ob_ref):
    y = jnp.dot(a_ref[...], w_ref[...], preferred_element_type=F32)
    out = _layer_norm_rows(ALPHA * x_ref[...] + y, g_ref[...], b_ref[...])
    of_ref[...] = out
    ob_ref[...] = out.astype(BF16)


def _proj_residual_ln(a, w, layer, x, g, b, tm=256):
    s, k = a.shape
    d = w.shape[2]
    tm = _tile(s, tm)
    row = lambda i: (i, 0)
    fixed = lambda i: (0, 0)
    return pl.pallas_call(
        _proj_ln_kernel,
        grid=(s // tm,),
        in_specs=[pl.BlockSpec((tm, k), row), pl.BlockSpec((None, k, d), lambda i: (layer, 0, 0)),
                  pl.BlockSpec((tm, d), row), pl.BlockSpec((1, d), fixed),
                  pl.BlockSpec((1, d), fixed)],
        out_specs=[pl.BlockSpec((tm, d), row), pl.BlockSpec((tm, d), row)],
        out_shape=[jax.ShapeDtypeStruct((s, d), F32), jax.ShapeDtypeStruct((s, d), BF16)],
        compiler_params=_params("parallel"),
        name="proj_residual_ln",
    )(a, w, x, g.reshape(1, d), b.reshape(1, d))


A_Q_COLS = A_HEADS * A_HEAD_DIM
A_PAIR_COLS = 2 * A_HEAD_DIM


def _swa_weight(w_qkv):
    n, d = w_qkv.shape[:2]
    wq = w_qkv[..., :A_Q_COLS]
    kv = (n, d, A_KV_HEADS, A_HEAD_DIM)
    wk = w_qkv[..., A_Q_COLS:A_Q_COLS + A_KV_HEADS * A_HEAD_DIM].reshape(kv)
    wv = w_qkv[..., A_Q_COLS + A_KV_HEADS * A_HEAD_DIM:].reshape(kv)
    dup = lambda w: jnp.concatenate([w, w], axis=-1).reshape(n, d, A_KV_HEADS * A_PAIR_COLS)
    return jnp.concatenate([wq, dup(wk), dup(wv)], axis=-1).astype(BF16)


def _half_masks(dtype):
    lane = lax.broadcasted_iota(jnp.int32, (1, LANES), 1)
    lo = (lane < LANES // 2).astype(F32)
    return lo.astype(dtype), (1.0 - lo).astype(dtype)


def _swa_kernel(q_ref, kp_ref, kc_ref, vp_ref, vc_ref, sink_ref, slope_ref, o_ref):
    i = pl.program_id(0)
    blk = WINDOW
    row = lax.broadcasted_iota(jnp.int32, (2 * blk, 2 * blk), 0)
    col = lax.broadcasted_iota(jnp.int32, (2 * blk, 2 * blk), 1)
    dist = jnp.bitwise_and(row, blk - 1) + blk - col
    valid = (dist >= 0) & (dist < WINDOW) & ((col >= blk) | (i > 0))
    distf = dist.astype(F32)
    top = row < blk
    top_col = top[:, :1]
    mlo, mhi = _half_masks(BF16)
    lo_f = lax.broadcasted_iota(jnp.int32, (blk, LANES), 1) < LANES // 2
    for g in range(A_KV_HEADS):
        kv = slice(g * A_PAIR_COLS, (g + 1) * A_PAIR_COLS)
        kband = jnp.concatenate([kp_ref[:, kv], kc_ref[:, kv]], axis=0)
        vband = jnp.concatenate([vp_ref[:, kv], vc_ref[:, kv]], axis=0)
        for p in range(A_GROUP // 2):
            h0 = g * A_GROUP + 2 * p
            qcols = slice((h0 // 2) * LANES, (h0 // 2 + 1) * LANES)
            q2 = q_ref[:, qcols] * (A_HEAD_DIM ** -0.5)
            lhs = jnp.concatenate([q2 * mlo, q2 * mhi], axis=0)
            s = lax.dot_general(lhs, kband, (((1,), (1,)), ((), ())),
                                preferred_element_type=F32)
            slope = jnp.where(top, slope_ref[h0], slope_ref[h0 + 1])
            sink = jnp.where(top_col, sink_ref[h0], sink_ref[h0 + 1])
            s = s - slope * distf
            s = jnp.where(valid, s, NEG)
            m = jnp.maximum(jnp.max(s, axis=-1, keepdims=True), sink)
            pexp = jnp.exp(s - m)
            denom = jnp.sum(pexp, axis=-1, keepdims=True) + jnp.exp(sink - m)
            probs = (pexp / denom).astype(BF16)
            o2 = jnp.dot(probs, vband, preferred_element_type=F32)
            o_ref[:, qcols] = jnp.where(lo_f, o2[:blk], o2[blk:]).astype(o_ref.dtype)


def _swa_attention(qkv, sinks):
    s = qkv.shape[0]
    nb = s // WINDOW
    prev = lambda i: jnp.maximum(i - 1, 0)
    smem = pl.BlockSpec(memory_space=pltpu.SMEM)
    kv_cols = A_KV_HEADS * A_PAIR_COLS
    k_blk, v_blk = A_Q_COLS // kv_cols, A_Q_COLS // kv_cols + 1
    return pl.pallas_call(
        _swa_kernel,
        grid=(nb,),
        in_specs=[pl.BlockSpec((WINDOW, A_Q_COLS), lambda i: (i, 0)),
                  pl.BlockSpec((WINDOW, kv_cols), lambda i: (prev(i), k_blk)),
                  pl.BlockSpec((WINDOW, kv_cols), lambda i: (i, k_blk)),
                  pl.BlockSpec((WINDOW, kv_cols), lambda i: (prev(i), v_blk)),
                  pl.BlockSpec((WINDOW, kv_cols), lambda i: (i, v_blk)),
                  smem, smem],
        out_specs=pl.BlockSpec((WINDOW, A_Q_COLS), lambda i: (i, 0)),
        out_shape=jax.ShapeDtypeStruct((s, A_Q_COLS), BF16),
        compiler_params=_params("parallel"),
        name="swa_attention",
    )(qkv, qkv, qkv, qkv, qkv, sinks.astype(F32), _alibi_slopes(A_HEADS))


B_QK_COLS = B_HEADS * 2 * B_HEAD_DIM
B_K_BLOCK0 = B_QK_COLS // B_V_DIM
B_V_BLOCK0 = 2 * B_K_BLOCK0


_ALIBI_COLS = 12
_POS_LOW = 256
_EXP_ZERO_BELOW = 110.0
_BOUND_SLACK = 1.001
_BOUND_MARGIN = 1.0


def _diff_head_tables(slope, lhs_scr, kx_scr, vx_scr, tq):
    sv = jnp.full((1, LANES), slope, F32)
    s1 = sv.astype(BF16).astype(F32)
    s2 = (sv - s1).astype(BF16).astype(F32)
    s3 = (sv - s1 - s2).astype(BF16).astype(F32)
    lane = lax.broadcasted_iota(jnp.int32, (1, LANES), 1)
    third = lambda k: (lane == k) | (lane == k + 3) | (lane == k + 6) | (lane == k + 9)
    piece = jnp.where(third(0), s1, jnp.where(third(1), s2, s3))

    row = lax.broadcasted_iota(jnp.int32, (2 * tq, LANES), 0)
    lane_q = lax.broadcasted_iota(jnp.int32, (2 * tq, LANES), 1)
    r = jnp.where(row >= tq, row - tq, row)
    r_lo = jnp.bitwise_and(r, _POS_LOW - 1).astype(F32)
    r_hi = jnp.where(r >= _POS_LOW, 1.0, 0.0)
    ext = jnp.where(lane_q < 3, r_lo, jnp.where(lane_q < 6, r_hi,
                    jnp.where(lane_q < _ALIBI_COLS, piece, 0.0)))
    lhs_scr[:, LANES:2 * LANES] = ext.astype(BF16)

    c = lax.broadcasted_iota(jnp.int32, (tq, LANES), 0)
    lane_k = lax.broadcasted_iota(jnp.int32, (tq, LANES), 1)
    c_lo = jnp.bitwise_and(c, _POS_LOW - 1).astype(F32)
    c_hi = jnp.where(c >= _POS_LOW, float(_POS_LOW), 0.0)
    kext = jnp.where(lane_k < 3, -piece, jnp.where(lane_k < 6, -float(_POS_LOW) * piece,
                     jnp.where(lane_k < 9, c_lo, jnp.where(lane_k < _ALIBI_COLS, c_hi, 0.0))))
    kx_scr[...] = kext.astype(BF16)
    vx_scr[...] = jnp.where(lane_k == 0, 1.0, 0.0).astype(BF16)


def _diff_kernel(q_ref, k_ref, v_ref, lq1_ref, lk1_ref, lq2_ref, lk2_ref, g_ref, slope_ref,
                 o_ref, lhs_scr, kx_scr, vx_scr, m_scr, acc_scr, s_scr, k2max_scr, *, tq,
                 lambda_init):
    h = pl.program_id(0)
    i = pl.program_id(1)
    slope = slope_ref[h]

    @pl.when(i == 0)
    def _head_init():
        _diff_head_tables(slope, lhs_scr, kx_scr, vx_scr, tq)
        k2 = jnp.square(k_ref[...].astype(F32))
        k2max = jnp.maximum(*[jnp.max(jnp.sum(k2 * msk, axis=-1, keepdims=True), axis=0,
                                      keepdims=True) for msk in _half_masks(F32)])
        k2max_scr[...] = jnp.broadcast_to(k2max, k2max_scr.shape)

    mlo, mhi = _half_masks(BF16)
    q = q_ref[...] * (B_HEAD_DIM ** -0.5)
    lhs_scr[0:tq, 0:LANES] = q * mlo
    lhs_scr[tq:2 * tq, 0:LANES] = q * mhi
    m_scr[...] = jnp.full(m_scr.shape, -jnp.inf, F32)
    acc_scr[...] = jnp.zeros(acc_scr.shape, F32)

    def scores(j, slot):
        start = pl.multiple_of(j * tq, tq)
        kx = jnp.concatenate([k_ref[pl.ds(start, tq), :], kx_scr[...]], axis=1)
        s_scr[slot] = lax.dot_general(lhs_scr[...], kx, (((1,), (1,)), ((), ())),
                                      preferred_element_type=F32)

    def accumulate(j, slot, diagonal):
        start = pl.multiple_of(j * tq, tq)
        vx = jnp.concatenate([v_ref[pl.ds(start, tq), :], vx_scr[...]], axis=1)
        for half in range(2):
            rows = slice(half * tq, (half + 1) * tq)
            s = s_scr[slot, rows, :]
            if diagonal:
                row = lax.broadcasted_iota(jnp.int32, (tq, tq), 0)
                col = lax.broadcasted_iota(jnp.int32, (tq, tq), 1)
                s = jnp.where(row >= col, s, NEG)
                off = 0.0
            else:
                off = slope * ((i - j) * tq).astype(F32)
            m_old = m_scr[rows, :]
            m_new = jnp.maximum(m_old, jnp.max(s, axis=-1, keepdims=True) - off)
            p = jnp.exp(s - (m_new + off))
            alpha = jnp.exp(m_old - m_new)
            acc_scr[rows, :] = alpha * acc_scr[rows, :] + jnp.dot(
                p.astype(BF16), vx, preferred_element_type=F32)
            m_scr[rows, :] = m_new

    scores(i, 0)
    scores(jnp.maximum(i - 1, 0), 1)
    accumulate(i, 0, True)
    qf = (q * mlo).astype(F32), (q * mhi).astype(F32)
    q2max = jnp.maximum(*[jnp.max(jnp.sum(c * c, axis=-1, keepdims=True), axis=0, keepdims=True)
                          for c in qf])
    m_min = jnp.min(m_scr[...], axis=0, keepdims=True)
    reach = (jnp.sqrt(q2max * k2max_scr[0:1, 0:1]) * _BOUND_SLACK + slope * (tq - 1)
             - m_min + _EXP_ZERO_BELOW + _BOUND_MARGIN) / (slope * tq)
    n_far = jnp.floor(jnp.minimum(reach, i.astype(F32))).astype(jnp.int32)[0, 0]

    def pair(jj, carry):
        j = i - 1 - 2 * jj
        scores(j - 1, 0)
        accumulate(j, 1, False)
        scores(jnp.maximum(j - 2, 0), 1)
        accumulate(j - 1, 0, False)
        return carry

    lax.fori_loop(0, jnp.right_shift(n_far, 1), pair, 0)

    @pl.when(jnp.bitwise_and(n_far, 1) == 1)
    def _odd_tail():
        accumulate(i - n_far, 1, False)

    lam = (jnp.exp(jnp.sum(lq1_ref[...] * lk1_ref[...], axis=-1, keepdims=True))
           - jnp.exp(jnp.sum(lq2_ref[...] * lk2_ref[...], axis=-1, keepdims=True))
           + lambda_init)
    o0 = acc_scr[0:tq, 0:B_V_DIM] / acc_scr[0:tq, B_V_DIM:B_V_DIM + 1]
    o1 = acc_scr[tq:2 * tq, 0:B_V_DIM] / acc_scr[tq:2 * tq, B_V_DIM:B_V_DIM + 1]
    o = o0 - lam * o1
    o = o * lax.rsqrt(jnp.mean(o * o, axis=-1, keepdims=True) + LN_EPS)
    o = o * g_ref[...] * (1.0 - lambda_init)
    o_ref[...] = o.astype(o_ref.dtype)


def _diff_attention(qkv, lq1, lk1, lq2, lk2, subln_g, lambda_init, tq=512):
    s = qkv.shape[0]
    tq = _tile(s, tq)
    assert tq <= 2 * _POS_LOW and tq % _POS_LOW == 0, tq
    vec = lambda n: pl.BlockSpec((1, n), lambda h, i: (0, 0))
    kern = functools.partial(_diff_kernel, tq=tq, lambda_init=lambda_init)
    return pl.pallas_call(
        kern,
        grid=(B_HEADS, s // tq),
        in_specs=[pl.BlockSpec((tq, B_V_DIM), lambda h, i: (i, h)),
                  pl.BlockSpec((s, B_V_DIM), lambda h, i: (0, B_K_BLOCK0 + h)),
                  pl.BlockSpec((s, B_V_DIM), lambda h, i: (0, B_V_BLOCK0 + h)),
                  vec(B_HEAD_DIM), vec(B_HEAD_DIM), vec(B_HEAD_DIM), vec(B_HEAD_DIM),
                  vec(B_V_DIM), pl.BlockSpec(memory_space=pltpu.SMEM)],
        out_specs=pl.BlockSpec((tq, B_V_DIM), lambda h, i: (i, h)),
        out_shape=jax.ShapeDtypeStruct((s, B_HEADS * B_V_DIM), BF16),
        scratch_shapes=[pltpu.VMEM((2 * tq, 2 * LANES), BF16), pltpu.VMEM((tq, LANES), BF16),
                        pltpu.VMEM((tq, LANES), BF16), pltpu.VMEM((2 * tq, 1), F32),
                        pltpu.VMEM((2 * tq, 2 * LANES), F32), pltpu.VMEM((2, 2 * tq, tq), F32),
                        pltpu.VMEM((8, LANES), F32)],
        compiler_params=_params("arbitrary", "arbitrary"),
        name="diff_attention",
    )(qkv, qkv, qkv, lq1.reshape(1, -1).astype(F32), lk1.reshape(1, -1).astype(F32),
      lq2.reshape(1, -1).astype(F32), lk2.reshape(1, -1).astype(F32),
      subln_g.reshape(1, -1).astype(F32), _alibi_slopes(B_HEADS))


_N_RANKED = P_TOPK + 1
_PAIR_RANKS = [(i, j) for i in range(_N_RANKED) for j in range(_N_RANKED)
               if (i + 1) * (j + 1) <= _N_RANKED]
SUBLANES = 8
_CAND_ROWS = 64
assert len(_PAIR_RANKS) <= _CAND_ROWS


def _sorting_network(n):
    pairs = []
    p = 1
    while p < n:
        k = p
        while k >= 1:
            for j in range(k % p, n - k, 2 * k):
                for i in range(min(k, n - j - k)):
                    if (i + j) // (2 * p) == (i + j + k) // (2 * p):
                        pairs.append((i + j, i + j + k))
            k //= 2
        p *= 2
    return pairs


def _top_values(work, count):
    n = work.shape[0] // SUBLANES
    r = [work[SUBLANES * k:SUBLANES * (k + 1), :] for k in range(n)]
    for lo, hi in _sorting_network(n):
        r[lo], r[hi] = jnp.maximum(r[lo], r[hi]), jnp.minimum(r[lo], r[hi])
    out = []
    for k in range(count):
        top = jnp.max(r[0], axis=0, keepdims=True)
        out.append(top)
        needed = count - 1 - k
        hit = r[0] == top
        for i in range(min(needed, n - 1)):
            r[i] = jnp.where(hit, r[i + 1], r[i])
        if needed >= n:
            r[n - 1] = jnp.where(hit, -jnp.inf, r[n - 1])
    return out


def _peer_score_kernel(q_ref, sk_ref, thr_ref, e1_ref, a2_ref, e2_ref, cand_scr):
    tt = q_ref.shape[0]
    cand_scr[...] = jnp.full(cand_scr.shape, -jnp.inf, F32)
    for h in range(P_HEADS):
        st = []
        for c in range(2):
            qs = q_ref[:, (2 * h + c) * P_HALF:(2 * h + c + 1) * P_HALF]
            st.append(lax.dot_general(sk_ref[h, c], qs, (((1,), (1,)), ((), ())),
                                      preferred_element_type=F32))
        a = _top_values(st[0], _N_RANKED)
        b = _top_values(st[1], _N_RANKED)
        for r, (i, j) in enumerate(_PAIR_RANKS):
            cand_scr[r:r + 1, :] = a[i] + b[j]
        best = _top_values(cand_scr[...], _N_RANKED)
        z = jnp.zeros((1, tt), F32)
        for v in best[:P_TOPK]:
            z = z + jnp.exp(v - best[0])
        cut = 0.5 * (best[P_TOPK - 1] + best[P_TOPK])
        thr_ref[h] = cut - st[0]
        a2_ref[h] = st[1]
        e1_ref[h] = jnp.exp(st[0] - a[0]) / z
        e2_ref[h] = jnp.exp(st[1] - b[0])


def _peer_scores(q, subkeys, layer, tt=512):
    s = q.shape[0]
    tt = _tile(s, tt)
    big = jax.ShapeDtypeStruct((P_HEADS, N_KEYS, s), F32)
    big_spec = pl.BlockSpec((P_HEADS, N_KEYS, tt), lambda t: (0, 0, t))
    return pl.pallas_call(
        _peer_score_kernel,
        grid=(s // tt,),
        in_specs=[pl.BlockSpec((tt, q.shape[1]), lambda t: (t, 0)),
                  pl.BlockSpec((None,) + subkeys.shape[1:], lambda t: (layer, 0, 0, 0, 0))],
        out_specs=[big_spec, big_spec, big_spec, big_spec],
        out_shape=[big, big, big, big],
        scratch_shapes=[pltpu.VMEM((_CAND_ROWS, tt), F32)],
        compiler_params=_params("parallel"),
        name="peer_scores",
    )(q, subkeys)


def _gelu(a):
    return 0.5 * a * (1.0 + lax.erf(a * np.float32(np.sqrt(0.5))))


def _peer_dense_kernel(x_ref, u_ref, v_ref, thr_ref, e1_ref, a2_ref, e2_ref, xres_ref, g_ref,
                       b_ref, of_ref, ob_ref, act_scr, h_scr, acc_scr, *, n_etiles):
    s = pl.program_id(0)
    _, te, tt = act_scr.shape
    e_prev = jnp.maximum(s - 1, 0) % n_etiles

    @pl.when(s == 0)
    def _first():
        act_scr[1] = jnp.zeros((te, tt), F32)

    @pl.when(e_prev == 0)
    def _init():
        acc_scr[...] = jnp.zeros(acc_scr.shape, F32)

    def body(write_slot, read_slot):
        act_scr[write_slot] = lax.dot_general(u_ref[...], x_ref[...], (((1,), (1,)), ((), ())),
                                              preferred_element_type=F32)
        for il in range(te // N_KEYS):
            rows = slice(il * N_KEYS, (il + 1) * N_KEYS)
            for lg in range(tt // LANES):
                cols = slice(lg * LANES, (lg + 1) * LANES)
                gate = jnp.zeros((N_KEYS, LANES), F32)
                for h in range(P_HEADS):
                    w = e2_ref[h, :, cols] * e1_ref[h, il:il + 1, cols]
                    gate = jnp.where(a2_ref[h, :, cols] > thr_ref[h, il:il + 1, cols],
                                     gate + w, gate)
                h_scr[rows, cols] = (gate * _gelu(act_scr[read_slot, rows, cols])).astype(BF16)
        acc_scr[...] += lax.dot_general(h_scr[...], v_ref[...], (((0,), (0,)), ((), ())),
                                        preferred_element_type=F32)

    @pl.when(s % 2 == 0)
    def _even():
        body(0, 1)

    @pl.when(s % 2 == 1)
    def _odd():
        body(1, 0)

    @pl.when((e_prev == n_etiles - 1) & (s > 0))
    def _finish():
        out = _layer_norm_rows(ALPHA * xres_ref[...] + acc_scr[...], g_ref[...], b_ref[...])
        of_ref[...] = out
        ob_ref[...] = out.astype(BF16)


def _peer_dense_ln(x_bf, u_bf, v_bf, layer, thr, e1, a2, e2, x_res, g, b, tt=512, te=1024):
    s, d = x_bf.shape
    n_exp = u_bf.shape[1]
    tt, te = _tile(s, tt), _tile(n_exp, te)
    ne = n_exp // te
    n_items = (s // tt) * ne
    cur = lambda i: jnp.minimum(i, n_items - 1)
    prev = lambda i: jnp.maximum(i - 1, 0)
    rows_spec = pl.BlockSpec((P_HEADS, te // N_KEYS, tt),
                             lambda i: (0, prev(i) % ne, prev(i) // ne))
    once = pl.Buffered(1)
    full_spec = pl.BlockSpec((P_HEADS, N_KEYS, tt), lambda i: (0, 0, prev(i) // ne),
                             pipeline_mode=once)
    tok_spec = pl.BlockSpec((tt, d), lambda i: (prev(i) // ne, 0))
    res_spec = pl.BlockSpec((tt, d), lambda i: (prev(i) // ne, 0), pipeline_mode=once)
    vec_spec = pl.BlockSpec((1, d), lambda i: (0, 0))
    return pl.pallas_call(
        functools.partial(_peer_dense_kernel, n_etiles=ne),
        grid=(n_items + 1,),
        in_specs=[pl.BlockSpec((tt, d), lambda i: (cur(i) // ne, 0)),
                  pl.BlockSpec((None, te, d), lambda i: (layer, cur(i) % ne, 0)),
                  pl.BlockSpec((None, te, d), lambda i: (layer, prev(i) % ne, 0)),
                  rows_spec, rows_spec, full_spec, full_spec, res_spec, vec_spec, vec_spec],
        out_specs=[tok_spec, tok_spec],
        out_shape=[jax.ShapeDtypeStruct((s, d), F32), jax.ShapeDtypeStruct((s, d), BF16)],
        scratch_shapes=[pltpu.VMEM((2, te, tt), F32), pltpu.VMEM((te, tt), BF16),
                        pltpu.VMEM((tt, d), F32)],
        compiler_params=_params("arbitrary"),
        name="peer_dense",
    )(x_bf, u_bf, v_bf, thr, e1, a2, e2, x_res, g.reshape(1, d), b.reshape(1, d))


def _peer_ffn_ln(x_f, x_bf, wq_bf, subkeys_bf, u_bf, v_bf, layer, g, b):
    q = _matmul(x_bf, wq_bf, layer, BF16)
    thr, e1, a2, e2 = _peer_scores(q, subkeys_bf, layer)
    return _peer_dense_ln(x_bf, u_bf, v_bf, layer, thr, e1, a2, e2, x_f, g, b)


def kernel(x, a_w_qkv, a_sinks, a_w_o, b_w_qkv, b_lambda_q1, b_lambda_k1, b_lambda_q2,
           b_lambda_k2, b_subln_g, b_w_o, ln1_g, ln1_b, ln2_g, ln2_b,
           peer_w_q, peer_subkeys, peer_u, peer_v):
    bsz, seq, d = x.shape
    xf = x.reshape(bsz * seq, d).astype(F32)
    assert bsz == 1, "attention kernels index one sequence"
    xb = xf.astype(BF16)
    u_bf, v_bf = peer_u.astype(BF16), peer_v.astype(BF16)
    wq_bf, sk_bf = peer_w_q.astype(BF16), peer_subkeys.astype(BF16)
    a_qkv_bf, a_o_bf = _swa_weight(a_w_qkv), a_w_o.astype(BF16)
    b_qkv_bf, b_o_bf = b_w_qkv.astype(BF16), b_w_o.astype(BF16)
    for i in range(DEPTH):
        j = i // 2
        if i % 2 == 0:
            qkv = _matmul(xb, a_qkv_bf, j, BF16)
            mix = _swa_attention(qkv, a_sinks[j])
            w_o = a_o_bf
        else:
            lambda_init = 0.8 - 0.6 * float(np.exp(-0.3 * i))
            qkv = _matmul(xb, b_qkv_bf, j, BF16)
            mix = _diff_attention(qkv, b_lambda_q1[j], b_lambda_k1[j], b_lambda_q2[j],
                                  b_lambda_k2[j], b_subln_g[j], lambda_init)
            w_o = b_o_bf
        xf, xb = _proj_residual_ln(mix, w_o, j, xf, ln1_g[i], ln1_b[i])
        xf, xb = _peer_ffn_ln(xf, xb, wq_bf, sk_bf, u_bf, v_bf, i, ln2_g[i], ln2_b[i])
    return xf.reshape(bsz, seq, d).astype(x.dtype)
```

```python
import functools

import jax
import jax.numpy as jnp
import numpy as np
from jax import lax
from jax.experimental import pallas as pl
from jax.experimental.pallas import tpu as pltpu

F32 = jnp.float32
BF16 = jnp.bfloat16

DEPTH = 4
A_HEADS, A_KV_HEADS, A_HEAD_DIM, WINDOW = 32, 4, 64, 128
A_GROUP = A_HEADS // A_KV_HEADS
B_HEADS, B_HEAD_DIM = 16, 64
B_V_DIM = 2 * B_HEAD_DIM
P_HEADS, N_KEYS, P_TOPK, P_HALF = 8, 128, 16, 128
LN_EPS = 1e-5
NEG = -1e30
ALPHA = (2.0 * DEPTH) ** 0.25

LANES = 128
V7X_VMEM_LIMIT_BYTES = 56 * 1024 * 1024


def _tile(n, pref):
    t = min(n, pref)
    assert n % t == 0, (n, t)
    return t


def _params(*sem, flags=None):
    return pltpu.CompilerParams(dimension_semantics=sem, vmem_limit_bytes=V7X_VMEM_LIMIT_BYTES,
                                flags=flags)


def _alibi_slopes(n):
    return jnp.asarray((2.0 ** (-8.0 * np.arange(1, n + 1) / n)).astype(np.float32))


def _mm_kernel(a_ref, b_ref, o_ref):
    o_ref[...] = jnp.dot(a_ref[...], b_ref[...], preferred_element_type=F32).astype(o_ref.dtype)


def _matmul(a, b, layer, out_dtype, tm=1024, tn=1024):
    m, k = a.shape
    n = b.shape[2]
    tm, tn = _tile(m, tm), _tile(n, tn)
    return pl.pallas_call(
        _mm_kernel,
        grid=(m // tm, n // tn),
        in_specs=[pl.BlockSpec((tm, k), lambda i, j: (i, 0)),
                  pl.BlockSpec((None, k, tn), lambda i, j: (layer, 0, j))],
        out_specs=pl.BlockSpec((tm, tn), lambda i, j: (i, j)),
        out_shape=jax.ShapeDtypeStruct((m, n), out_dtype),
        compiler_params=_params("parallel", "parallel"),
        name="matmul",
    )(a, b)


def _layer_norm_rows(z, g, b):
    mu = jnp.mean(z, axis=-1, keepdims=True)
    zc = z - mu
    var = jnp.mean(zc * zc, axis=-1, keepdims=True)
    return zc * lax.rsqrt(var + LN_EPS) * g + b


def _proj_ln_kernel(a_ref, w_ref, x_ref, g_ref, b_ref, of_ref, ob_ref):
    y = jnp.dot(a_ref[...], w_ref[...], preferred_element_type=F32)
    out = _layer_norm_rows(ALPHA * x_ref[...] + y, g_ref[...], b_ref[...])
    of_ref[...] = out
    ob_ref[...] = out.astype(BF16)


def _proj_residual_ln(a, w, layer, x, g, b, tm=256):
    s, k = a.shape
    d = w.shape[2]
    tm = _tile(s, tm)
    row = lambda i: (i, 0)
    fixed = lambda i: (0, 0)
    return pl.pallas_call(
        _proj_ln_kernel,
        grid=(s // tm,),
        in_specs=[pl.BlockSpec((tm, k), row), pl.BlockSpec((None, k, d), lambda i: (layer, 0, 0)),
                  pl.BlockSpec((tm, d), row), pl.BlockSpec((1, d), fixed),
                  pl.BlockSpec((1, d), fixed)],
        out_specs=[pl.BlockSpec((tm, d), row), pl.BlockSpec((tm, d), row)],
        out_shape=[jax.ShapeDtypeStruct((s, d), F32), jax.ShapeDtypeStruct((s, d), BF16)],
        compiler_params=_params("parallel"),
        name="proj_residual_ln",
    )(a, w, x, g.reshape(1, d), b.reshape(1, d))


A_Q_COLS = A_HEADS * A_HEAD_DIM
A_PAIR_COLS = 2 * A_HEAD_DIM


def _swa_weight(w_qkv):
    n, d = w_qkv.shape[:2]
    wq = w_qkv[..., :A_Q_COLS]
    kv = (n, d, A_KV_HEADS, A_HEAD_DIM)
    wk = w_qkv[..., A_Q_COLS:A_Q_COLS + A_KV_HEADS * A_HEAD_DIM].reshape(kv)
    wv = w_qkv[..., A_Q_COLS + A_KV_HEADS * A_HEAD_DIM:].reshape(kv)
    dup = lambda w: jnp.concatenate([w, w], axis=-1).reshape(n, d, A_KV_HEADS * A_PAIR_COLS)
    return jnp.concatenate([wq, dup(wk), dup(wv)], axis=-1).astype(BF16)


def _half_masks(dtype):
    lane = lax.broadcasted_iota(jnp.int32, (1, LANES), 1)
    lo = (lane < LANES // 2).astype(F32)
    return lo.astype(dtype), (1.0 - lo).astype(dtype)


def _swa_kernel(q_ref, kp_ref, kc_ref, vp_ref, vc_ref, sink_ref, slope_ref, o_ref):
    i = pl.program_id(0)
    blk = WINDOW
    row = lax.broadcasted_iota(jnp.int32, (2 * blk, 2 * blk), 0)
    col = lax.broadcasted_iota(jnp.int32, (2 * blk, 2 * blk), 1)
    dist = jnp.bitwise_and(row, blk - 1) + blk - col
    valid = (dist >= 0) & (dist < WINDOW) & ((col >= blk) | (i > 0))
    distf = dist.astype(F32)
    top = row < blk
    top_col = top[:, :1]
    mlo, mhi = _half_masks(BF16)
    lo_f = lax.broadcasted_iota(jnp.int32, (blk, LANES), 1) < LANES // 2
    for g in range(A_KV_HEADS):
        kv = slice(g * A_PAIR_COLS, (g + 1) * A_PAIR_COLS)
        kband = jnp.concatenate([kp_ref[:, kv], kc_ref[:, kv]], axis=0)
        vband = jnp.concatenate([vp_ref[:, kv], vc_ref[:, kv]], axis=0)
        for p in range(A_GROUP // 2):
            h0 = g * A_GROUP + 2 * p
            qcols = slice((h0 // 2) * LANES, (h0 // 2 + 1) * LANES)
            q2 = q_ref[:, qcols] * (A_HEAD_DIM ** -0.5)
            lhs = jnp.concatenate([q2 * mlo, q2 * mhi], axis=0)
            s = lax.dot_general(lhs, kband, (((1,), (1,)), ((), ())),
                                preferred_element_type=F32)
            slope = jnp.where(top, slope_ref[h0], slope_ref[h0 + 1])
            sink = jnp.where(top_col, sink_ref[h0], sink_ref[h0 + 1])
            s = s - slope * distf
            s = jnp.where(valid, s, NEG)
            m = jnp.maximum(jnp.max(s, axis=-1, keepdims=True), sink)
            pexp = jnp.exp(s - m)
            denom = jnp.sum(pexp, axis=-1, keepdims=True) + jnp.exp(sink - m)
            probs = (pexp / denom).astype(BF16)
            o2 = jnp.dot(probs, vband, preferred_element_type=F32)
            o_ref[:, qcols] = jnp.where(lo_f, o2[:blk], o2[blk:]).astype(o_ref.dtype)


def _swa_attention(qkv, sinks):
    s = qkv.shape[0]
    nb = s // WINDOW
    prev = lambda i: jnp.maximum(i - 1, 0)
    smem = pl.BlockSpec(memory_space=pltpu.SMEM)
    kv_cols = A_KV_HEADS * A_PAIR_COLS
    k_blk, v_blk = A_Q_COLS // kv_cols, A_Q_COLS // kv_cols + 1
    return pl.pallas_call(
        _swa_kernel,
        grid=(nb,),
        in_specs=[pl.BlockSpec((WINDOW, A_Q_COLS), lambda i: (i, 0)),
                  pl.BlockSpec((WINDOW, kv_cols), lambda i: (prev(i), k_blk)),
                  pl.BlockSpec((WINDOW, kv_cols), lambda i: (i, k_blk)),
                  pl.BlockSpec((WINDOW, kv_cols), lambda i: (prev(i), v_blk)),
                  pl.BlockSpec((WINDOW, kv_cols), lambda i: (i, v_blk)),
                  smem, smem],
        out_specs=pl.BlockSpec((WINDOW, A_Q_COLS), lambda i: (i, 0)),
        out_shape=jax.ShapeDtypeStruct((s, A_Q_COLS), BF16),
        compiler_params=_params("parallel"),
        name="swa_attention",
    )(qkv, qkv, qkv, qkv, qkv, sinks.astype(F32), _alibi_slopes(A_HEADS))


B_QK_COLS = B_HEADS * 2 * B_HEAD_DIM
B_K_BLOCK0 = B_QK_COLS // B_V_DIM
B_V_BLOCK0 = 2 * B_K_BLOCK0


_ALIBI_COLS = 12
_POS_LOW = 256
_EXP_ZERO_BELOW = 110.0
_BOUND_SLACK = 1.001
_BOUND_MARGIN = 1.0


def _diff_head_tables(slope, lhs_scr, kx_scr, vx_scr, tq):
    sv = jnp.full((1, LANES), slope, F32)
    s1 = sv.astype(BF16).astype(F32)
    s2 = (sv - s1).astype(BF16).astype(F32)
    s3 = (sv - s1 - s2).astype(BF16).astype(F32)
    lane = lax.broadcasted_iota(jnp.int32, (1, LANES), 1)
    third = lambda k: (lane == k) | (lane == k + 3) | (lane == k + 6) | (lane == k + 9)
    piece = jnp.where(third(0), s1, jnp.where(third(1), s2, s3))

    row = lax.broadcasted_iota(jnp.int32, (2 * tq, LANES), 0)
    lane_q = lax.broadcasted_iota(jnp.int32, (2 * tq, LANES), 1)
    r = jnp.where(row >= tq, row - tq, row)
    r_lo = jnp.bitwise_and(r, _POS_LOW - 1).astype(F32)
    r_hi = jnp.where(r >= _POS_LOW, 1.0, 0.0)
    ext = jnp.where(lane_q < 3, r_lo, jnp.where(lane_q < 6, r_hi,
                    jnp.where(lane_q < _ALIBI_COLS, piece, 0.0)))
    lhs_scr[:, LANES:2 * LANES] = ext.astype(BF16)

    c = lax.broadcasted_iota(jnp.int32, (tq, LANES), 0)
    lane_k = lax.broadcasted_iota(jnp.int32, (tq, LANES), 1)
    c_lo = jnp.bitwise_and(c, _POS_LOW - 1).astype(F32)
    c_hi = jnp.where(c >= _POS_LOW, float(_POS_LOW), 0.0)
    kext = jnp.where(lane_k < 3, -piece, jnp.where(lane_k < 6, -float(_POS_LOW) * piece,
                     jnp.where(lane_k < 9, c_lo, jnp.where(lane_k < _ALIBI_COLS, c_hi, 0.0))))
    kx_scr[...] = kext.astype(BF16)
    vx_scr[...] = jnp.where(lane_k == 0, 1.0, 0.0).astype(BF16)


def _diff_kernel(q_ref, k_ref, v_ref, lq1_ref, lk1_ref, lq2_ref, lk2_ref, g_ref, slope_ref,
                 o_ref, lhs_scr, kx_scr, vx_scr, m_scr, acc_scr, s_scr, k2max_scr, *, tq,
                 lambda_init):
    h = pl.program_id(0)
    i = pl.program_id(1)
    slope = slope_ref[h]

    @pl.when(i == 0)
    def _head_init():
        _diff_head_tables(slope, lhs_scr, kx_scr, vx_scr, tq)
        k2 = jnp.square(k_ref[...].astype(F32))
        k2max = jnp.maximum(*[jnp.max(jnp.sum(k2 * msk, axis=-1, keepdims=True), axis=0,
                                      keepdims=True) for msk in _half_masks(F32)])
        k2max_scr[...] = jnp.broadcast_to(k2max, k2max_scr.shape)

    mlo, mhi = _half_masks(BF16)
    q = q_ref[...] * (B_HEAD_DIM ** -0.5)
    lhs_scr[0:tq, 0:LANES] = q * mlo
    lhs_scr[tq:2 * tq, 0:LANES] = q * mhi
    m_scr[...] = jnp.full(m_scr.shape, -jnp.inf, F32)
    acc_scr[...] = jnp.zeros(acc_scr.shape, F32)

    def scores(j, slot):
        start = pl.multiple_of(j * tq, tq)
        kx = jnp.concatenate([k_ref[pl.ds(start, tq), :], kx_scr[...]], axis=1)
        s_scr[slot] = lax.dot_general(lhs_scr[...], kx, (((1,), (1,)), ((), ())),
                                      preferred_element_type=F32)

    def accumulate(j, slot, diagonal):
        start = pl.multiple_of(j * tq, tq)
        vx = jnp.concatenate([v_ref[pl.ds(start, tq), :], vx_scr[...]], axis=1)
        for half in range(2):
            rows = slice(half * tq, (half + 1) * tq)
            s = s_scr[slot, rows, :]
            if diagonal:
                row = lax.broadcasted_iota(jnp.int32, (tq, tq), 0)
                col = lax.broadcasted_iota(jnp.int32, (tq, tq), 1)
                s = jnp.where(row >= col, s, NEG)
                off = 0.0
            else:
                off = slope * ((i - j) * tq).astype(F32)
            m_old = m_scr[rows, :]
            m_new = jnp.maximum(m_old, jnp.max(s, axis=-1, keepdims=True) - off)
            p = jnp.exp(s - (m_new + off))
            alpha = jnp.exp(m_old - m_new)
            acc_scr[rows, :] = alpha * acc_scr[rows, :] + jnp.dot(
                p.astype(BF16), vx, preferred_element_type=F32)
            m_scr[rows, :] = m_new

    scores(i, 0)
    scores(jnp.maximum(i - 1, 0), 1)
    accumulate(i, 0, True)
    qf = (q * mlo).astype(F32), (q * mhi).astype(F32)
    q2max = jnp.maximum(*[jnp.max(jnp.sum(c * c, axis=-1, keepdims=True), axis=0, keepdims=True)
                          for c in qf])
    m_min = jnp.min(m_scr[...], axis=0, keepdims=True)
    reach = (jnp.sqrt(q2max * k2max_scr[0:1, 0:1]) * _BOUND_SLACK + slope * (tq - 1)
             - m_min + _EXP_ZERO_BELOW + _BOUND_MARGIN) / (slope * tq)
    n_far = jnp.floor(jnp.minimum(reach, i.astype(F32))).astype(jnp.int32)[0, 0]

    def pair(jj, carry):
        j = i - 1 - 2 * jj
        scores(j - 1, 0)
        accumulate(j, 1, False)
        scores(jnp.maximum(j - 2, 0), 1)
        accumulate(j - 1, 0, False)
        return carry

    lax.fori_loop(0, jnp.right_shift(n_far, 1), pair, 0)

    @pl.when(jnp.bitwise_and(n_far, 1) == 1)
    def _odd_tail():
        accumulate(i - n_far, 1, False)

    lam = (jnp.exp(jnp.sum(lq1_ref[...] * lk1_ref[...], axis=-1, keepdims=True))
           - jnp.exp(jnp.sum(lq2_ref[...] * lk2_ref[...], axis=-1, keepdims=True))
           + lambda_init)
    o0 = acc_scr[0:tq, 0:B_V_DIM] / acc_scr[0:tq, B_V_DIM:B_V_DIM + 1]
    o1 = acc_scr[tq:2 * tq, 0:B_V_DIM] / acc_scr[tq:2 * tq, B_V_DIM:B_V_DIM + 1]
    o = o0 - lam * o1
    o = o * lax.rsqrt(jnp.mean(o * o, axis=-1, keepdims=True) + LN_EPS)
    o = o * g_ref[...] * (1.0 - lambda_init)
    o_ref[...] = o.astype(o_ref.dtype)


def _diff_attention(qkv, lq1, lk1, lq2, lk2, subln_g, lambda_init, tq=512):
    s = qkv.shape[0]
    tq = _tile(s, tq)
    assert tq <= 2 * _POS_LOW and tq % _POS_LOW == 0, tq
    vec = lambda n: pl.BlockSpec((1, n), lambda h, i: (0, 0))
    kern = functools.partial(_diff_kernel, tq=tq, lambda_init=lambda_init)
    return pl.pallas_call(
        kern,
        grid=(B_HEADS, s // tq),
        in_specs=[pl.BlockSpec((tq, B_V_DIM), lambda h, i: (i, h)),
                  pl.BlockSpec((s, B_V_DIM), lambda h, i: (0, B_K_BLOCK0 + h)),
                  pl.BlockSpec((s, B_V_DIM), lambda h, i: (0, B_V_BLOCK0 + h)),
                  vec(B_HEAD_DIM), vec(B_HEAD_DIM), vec(B_HEAD_DIM), vec(B_HEAD_DIM),
                  vec(B_V_DIM), pl.BlockSpec(memory_space=pltpu.SMEM)],
        out_specs=pl.BlockSpec((tq, B_V_DIM), lambda h, i: (i, h)),
        out_shape=jax.ShapeDtypeStruct((s, B_HEADS * B_V_DIM), BF16),
        scratch_shapes=[pltpu.VMEM((2 * tq, 2 * LANES), BF16), pltpu.VMEM((tq, LANES), BF16),
                        pltpu.VMEM((tq, LANES), BF16), pltpu.VMEM((2 * tq, 1), F32),
                        pltpu.VMEM((2 * tq, 2 * LANES), F32), pltpu.VMEM((2, 2 * tq, tq), F32),
                        pltpu.VMEM((8, LANES), F32)],
        compiler_params=_params("arbitrary", "arbitrary"),
        name="diff_attention",
    )(qkv, qkv, qkv, lq1.reshape(1, -1).astype(F32), lk1.reshape(1, -1).astype(F32),
      lq2.reshape(1, -1).astype(F32), lk2.reshape(1, -1).astype(F32),
      subln_g.reshape(1, -1).astype(F32), _alibi_slopes(B_HEADS))


_N_RANKED = P_TOPK + 1
_PAIR_RANKS = [(i, j) for i in range(_N_RANKED) for j in range(_N_RANKED)
               if (i + 1) * (j + 1) <= _N_RANKED]
SUBLANES = 8
_CAND_ROWS = 64
assert len(_PAIR_RANKS) <= _CAND_ROWS


def _sorting_network(n):
    pairs = []
    p = 1
    while p < n:
        k = p
        while k >= 1:
            for j in range(k % p, n - k, 2 * k):
                for i in range(min(k, n - j - k)):
                    if (i + j) // (2 * p) == (i + j + k) // (2 * p):
                        pairs.append((i + j, i + j + k))
            k //= 2
        p *= 2
    return pairs


def _top_values(work, count):
    n = work.shape[0] // SUBLANES
    r = [work[SUBLANES * k:SUBLANES * (k + 1), :] for k in range(n)]
    for lo, hi in _sorting_network(n):
        r[lo], r[hi] = jnp.maximum(r[lo], r[hi]), jnp.minimum(r[lo], r[hi])
    out = []
    for k in range(count):
        top = jnp.max(r[0], axis=0, keepdims=True)
        out.append(top)
        needed = count - 1 - k
        hit = r[0] == top
        for i in range(min(needed, n - 1)):
            r[i] = jnp.where(hit, r[i + 1], r[i])
        if needed >= n:
            r[n - 1] = jnp.where(hit, -jnp.inf, r[n - 1])
    return out


def _peer_score_kernel(x_ref, wq_ref, sk_ref, thr_ref, e1_ref, a2_ref, e2_ref, cand_scr, q_ref):
    tt = x_ref.shape[0]
    q_ref[...] = jnp.dot(x_ref[...], wq_ref[...], preferred_element_type=F32).astype(q_ref.dtype)
    cand_scr[...] = jnp.full(cand_scr.shape, -jnp.inf, F32)
    for h in range(P_HEADS):
        st = []
        for c in range(2):
            qs = q_ref[:, (2 * h + c) * P_HALF:(2 * h + c + 1) * P_HALF]
            st.append(lax.dot_general(sk_ref[h, c], qs, (((1,), (1,)), ((), ())),
                                      preferred_element_type=F32))
        a = _top_values(st[0], _N_RANKED)
        b = _top_values(st[1], _N_RANKED)
        for r, (i, j) in enumerate(_PAIR_RANKS):
            cand_scr[r:r + 1, :] = a[i] + b[j]
        best = _top_values(cand_scr[...], _N_RANKED)
        z = jnp.zeros((1, tt), F32)
        for v in best[:P_TOPK]:
            z = z + jnp.exp(v - best[0])
        cut = 0.5 * (best[P_TOPK - 1] + best[P_TOPK])
        thr_ref[h] = cut - st[0]
        a2_ref[h] = st[1]
        e1_ref[h] = jnp.exp(st[0] - a[0]) / z
        e2_ref[h] = jnp.exp(st[1] - b[0])


def _peer_scores(x_bf, w_q, subkeys, layer, tt=512):
    s, d = x_bf.shape
    dq = w_q.shape[2]
    tt = _tile(s, tt)
    big = jax.ShapeDtypeStruct((P_HEADS, N_KEYS, s), F32)
    big_spec = pl.BlockSpec((P_HEADS, N_KEYS, tt), lambda t: (0, 0, t))
    return pl.pallas_call(
        _peer_score_kernel,
        grid=(s // tt,),
        in_specs=[pl.BlockSpec((tt, d), lambda t: (t, 0)),
                  pl.BlockSpec((None, d, dq), lambda t: (layer, 0, 0)),
                  pl.BlockSpec((None,) + subkeys.shape[1:], lambda t: (layer, 0, 0, 0, 0))],
        out_specs=[big_spec, big_spec, big_spec, big_spec],
        out_shape=[big, big, big, big],
        scratch_shapes=[pltpu.VMEM((_CAND_ROWS, tt), F32), pltpu.VMEM((tt, dq), BF16)],
        compiler_params=_params("parallel"),
        name="peer_scores",
    )(x_bf, w_q, subkeys)


def _gelu(a):
    return 0.5 * a * (1.0 + lax.erf(a * np.float32(np.sqrt(0.5))))


def _peer_dense_kernel(x_ref, u_ref, v_ref, thr_ref, e1_ref, a2_ref, e2_ref, xres_ref, g_ref,
                       b_ref, of_ref, ob_ref, act_scr, h_scr, acc_scr, *, n_etiles):
    s = pl.program_id(0)
    _, te, tt = act_scr.shape
    e_prev = jnp.maximum(s - 1, 0) % n_etiles

    @pl.when(s == 0)
    def _first():
        act_scr[1] = jnp.zeros((te, tt), F32)

    @pl.when(e_prev == 0)
    def _init():
        acc_scr[...] = jnp.zeros(acc_scr.shape, F32)

    def body(write_slot, read_slot):
        act_scr[write_slot] = lax.dot_general(u_ref[...], x_ref[...], (((1,), (1,)), ((), ())),
                                              preferred_element_type=F32)
        for il in range(te // N_KEYS):
            rows = slice(il * N_KEYS, (il + 1) * N_KEYS)
            for lg in range(tt // LANES):
                cols = slice(lg * LANES, (lg + 1) * LANES)
                gate = jnp.zeros((N_KEYS, LANES), F32)
                for h in range(P_HEADS):
                    w = e2_ref[h, :, cols] * e1_ref[h, il:il + 1, cols]
                    gate = jnp.where(a2_ref[h, :, cols] > thr_ref[h, il:il + 1, cols],
                                     gate + w, gate)
                h_scr[rows, cols] = (gate * _gelu(act_scr[read_slot, rows, cols])).astype(BF16)
        acc_scr[...] += lax.dot_general(h_scr[...], v_ref[...], (((0,), (0,)), ((), ())),
                                        preferred_element_type=F32)

    @pl.when(s % 2 == 0)
    def _even():
        body(0, 1)

    @pl.when(s % 2 == 1)
    def _odd():
        body(1, 0)

    @pl.when((e_prev == n_etiles - 1) & (s > 0))
    def _finish():
        out = _layer_norm_rows(ALPHA * xres_ref[...] + acc_scr[...], g_ref[...], b_ref[...])
        of_ref[...] = out
        ob_ref[...] = out.astype(BF16)


def _peer_dense_ln(x_bf, u_bf, v_bf, layer, thr, e1, a2, e2, x_res, g, b, tt=512, te=1024):
    s, d = x_bf.shape
    n_exp = u_bf.shape[1]
    tt, te = _tile(s, tt), _tile(n_exp, te)
    ne = n_exp // te
    n_items = (s // tt) * ne
    cur = lambda i: jnp.minimum(i, n_items - 1)
    prev = lambda i: jnp.maximum(i - 1, 0)
    rows_spec = pl.BlockSpec((P_HEADS, te // N_KEYS, tt),
                             lambda i: (0, prev(i) % ne, prev(i) // ne))
    once = pl.Buffered(1)
    full_spec = pl.BlockSpec((P_HEADS, N_KEYS, tt), lambda i: (0, 0, prev(i) // ne),
                             pipeline_mode=once)
    tok_spec = pl.BlockSpec((tt, d), lambda i: (prev(i) // ne, 0))
    res_spec = pl.BlockSpec((tt, d), lambda i: (prev(i) // ne, 0), pipeline_mode=once)
    vec_spec = pl.BlockSpec((1, d), lambda i: (0, 0))
    return pl.pallas_call(
        functools.partial(_peer_dense_kernel, n_etiles=ne),
        grid=(n_items + 1,),
        in_specs=[pl.BlockSpec((tt, d), lambda i: (cur(i) // ne, 0)),
                  pl.BlockSpec((None, te, d), lambda i: (layer, cur(i) % ne, 0)),
                  pl.BlockSpec((None, te, d), lambda i: (layer, prev(i) % ne, 0)),
                  rows_spec, rows_spec, full_spec, full_spec, res_spec, vec_spec, vec_spec],
        out_specs=[tok_spec, tok_spec],
        out_shape=[jax.ShapeDtypeStruct((s, d), F32), jax.ShapeDtypeStruct((s, d), BF16)],
        scratch_shapes=[pltpu.VMEM((2, te, tt), F32), pltpu.VMEM((te, tt), BF16),
                        pltpu.VMEM((tt, d), F32)],
        compiler_params=_params("arbitrary"),
        name="peer_dense",
    )(x_bf, u_bf, v_bf, thr, e1, a2, e2, x_res, g.reshape(1, d), b.reshape(1, d))


def _peer_ffn_ln(x_f, x_bf, wq_bf, subkeys_bf, u_bf, v_bf, layer, g, b):
    thr, e1, a2, e2 = _peer_scores(x_bf, wq_bf, subkeys_bf, layer)
    return _peer_dense_ln(x_bf, u_bf, v_bf, layer, thr, e1, a2, e2, x_f, g, b)


def kernel(x, a_w_qkv, a_sinks, a_w_o, b_w_qkv, b_lambda_q1, b_lambda_k1, b_lambda_q2,
           b_lambda_k2, b_subln_g, b_w_o, ln1_g, ln1_b, ln2_g, ln2_b,
           peer_w_q, peer_subkeys, peer_u, peer_v):
    bsz, seq, d = x.shape
    xf = x.reshape(bsz * seq, d).astype(F32)
    assert bsz == 1, "attention kernels index one sequence"
    xb = xf.astype(BF16)
    u_bf, v_bf = peer_u.astype(BF16), peer_v.astype(BF16)
    wq_bf, sk_bf = peer_w_q.astype(BF16), peer_subkeys.astype(BF16)
    a_qkv_bf, a_o_bf = _swa_weight(a_w_qkv), a_w_o.astype(BF16)
    b_qkv_bf, b_o_bf = b_w_qkv.astype(BF16), b_w_o.astype(BF16)
    for i in range(DEPTH):
        j = i // 2
        if i % 2 == 0:
            qkv = _matmul(xb, a_qkv_bf, j, BF16)
            mix = _swa_attention(qkv, a_sinks[j])
            w_o = a_o_bf
        else:
            lambda_init = 0.8 - 0.6 * float(np.exp(-0.3 * i))
            qkv = _matmul(xb, b_qkv_bf, j, BF16)
            mix = _diff_attention(qkv, b_lambda_q1[j], b_lambda_k1[j], b_lambda_q2[j],
                                  b_lambda_k2[j], b_subln_g[j], lambda_init)
            w_o = b_o_bf
        xf, xb = _proj_residual_ln(mix, w_o, j, xf, ln1_g[i], ln1_b[i])
        xf, xb = _peer_ffn_ln(xf, xb, wq_bf, sk_bf, u_bf, v_bf, i, ln2_g[i], ln2_b[i])
    return xf.reshape(bsz, seq, d).astype(x.dtype)
```

```python
import functools

import jax
import jax.numpy as jnp
import numpy as np
from jax import lax
from jax.experimental import pallas as pl
from jax.experimental.pallas import tpu as pltpu

F32 = jnp.float32
BF16 = jnp.bfloat16

DEPTH = 4
A_HEADS, A_KV_HEADS, A_HEAD_DIM, WINDOW = 32, 4, 64, 128
A_GROUP = A_HEADS // A_KV_HEADS
B_HEADS, B_HEAD_DIM = 16, 64
B_V_DIM = 2 * B_HEAD_DIM
P_HEADS, N_KEYS, P_TOPK, P_HALF = 8, 128, 16, 128
LN_EPS = 1e-5
NEG = -1e30
ALPHA = (2.0 * DEPTH) ** 0.25

LANES = 128
V7X_VMEM_LIMIT_BYTES = 56 * 1024 * 1024


def _tile(n, pref):
    t = min(n, pref)
    assert n % t == 0, (n, t)
    return t


def _params(*sem, flags=None):
    return pltpu.CompilerParams(dimension_semantics=sem, vmem_limit_bytes=V7X_VMEM_LIMIT_BYTES,
                                flags=flags)


def _alibi_slopes(n):
    return jnp.asarray((2.0 ** (-8.0 * np.arange(1, n + 1) / n)).astype(np.float32))


def _mm_kernel(a_ref, b_ref, o_ref):
    o_ref[...] = jnp.dot(a_ref[...], b_ref[...], preferred_element_type=F32).astype(o_ref.dtype)


def _matmul(a, b, layer, out_dtype, tm=2048, tn=1024):
    m, k = a.shape
    n = b.shape[2]
    tm, tn = _tile(m, tm), _tile(n, tn)
    return pl.pallas_call(
        _mm_kernel,
        grid=(m // tm, n // tn),
        in_specs=[pl.BlockSpec((tm, k), lambda i, j: (i, 0)),
                  pl.BlockSpec((None, k, tn), lambda i, j: (layer, 0, j))],
        out_specs=pl.BlockSpec((tm, tn), lambda i, j: (i, j)),
        out_shape=jax.ShapeDtypeStruct((m, n), out_dtype),
        compiler_params=_params("parallel", "parallel"),
        name="matmul",
    )(a, b)


def _layer_norm_rows(z, g, b):
    mu = jnp.mean(z, axis=-1, keepdims=True)
    zc = z - mu
    var = jnp.mean(zc * zc, axis=-1, keepdims=True)
    return zc * lax.rsqrt(var + LN_EPS) * g + b


def _proj_ln_kernel(a_ref, w_ref, x_ref, g_ref, b_ref, of_ref, ob_ref):
    y = jnp.dot(a_ref[...], w_ref[...], preferred_element_type=F32)
    out = _layer_norm_rows(ALPHA * x_ref[...] + y, g_ref[...], b_ref[...])
    of_ref[...] = out
    ob_ref[...] = out.astype(BF16)


def _proj_residual_ln(a, w, layer, x, g, b, tm=256):
    s, k = a.shape
    d = w.shape[2]
    tm = _tile(s, tm)
    row = lambda i: (i, 0)
    fixed = lambda i: (0, 0)
    return pl.pallas_call(
        _proj_ln_kernel,
        grid=(s // tm,),
        in_specs=[pl.BlockSpec((tm, k), row), pl.BlockSpec((None, k, d), lambda i: (layer, 0, 0)),
                  pl.BlockSpec((tm, d), row), pl.BlockSpec((1, d), fixed),
                  pl.BlockSpec((1, d), fixed)],
        out_specs=[pl.BlockSpec((tm, d), row), pl.BlockSpec((tm, d), row)],
        out_shape=[jax.ShapeDtypeStruct((s, d), F32), jax.ShapeDtypeStruct((s, d), BF16)],
        compiler_params=_params("parallel"),
        name="proj_residual_ln",
    )(a, w, x, g.reshape(1, d), b.reshape(1, d))


A_Q_COLS = A_HEADS * A_HEAD_DIM
A_PAIR_COLS = 2 * A_HEAD_DIM


def _swa_weight(w_qkv):
    n, d = w_qkv.shape[:2]
    wq = w_qkv[..., :A_Q_COLS]
    kv = (n, d, A_KV_HEADS, A_HEAD_DIM)
    wk = w_qkv[..., A_Q_COLS:A_Q_COLS + A_KV_HEADS * A_HEAD_DIM].reshape(kv)
    wv = w_qkv[..., A_Q_COLS + A_KV_HEADS * A_HEAD_DIM:].reshape(kv)
    dup = lambda w: jnp.concatenate([w, w], axis=-1).reshape(n, d, A_KV_HEADS * A_PAIR_COLS)
    return jnp.concatenate([wq, dup(wk), dup(wv)], axis=-1).astype(BF16)


def _half_masks(dtype):
    lane = lax.broadcasted_iota(jnp.int32, (1, LANES), 1)
    lo = (lane < LANES // 2).astype(F32)
    return lo.astype(dtype), (1.0 - lo).astype(dtype)


def _swa_kernel(q_ref, kp_ref, kc_ref, vp_ref, vc_ref, sink_ref, slope_ref, o_ref):
    i = pl.program_id(0)
    blk = WINDOW
    row = lax.broadcasted_iota(jnp.int32, (2 * blk, 2 * blk), 0)
    col = lax.broadcasted_iota(jnp.int32, (2 * blk, 2 * blk), 1)
    dist = jnp.bitwise_and(row, blk - 1) + blk - col
    valid = (dist >= 0) & (dist < WINDOW) & ((col >= blk) | (i > 0))
    distf = dist.astype(F32)
    top = row < blk
    top_col = top[:, :1]
    mlo, mhi = _half_masks(BF16)
    lo_f = lax.broadcasted_iota(jnp.int32, (blk, LANES), 1) < LANES // 2
    for g in range(A_KV_HEADS):
        kv = slice(g * A_PAIR_COLS, (g + 1) * A_PAIR_COLS)
        kband = jnp.concatenate([kp_ref[:, kv], kc_ref[:, kv]], axis=0)
        vband = jnp.concatenate([vp_ref[:, kv], vc_ref[:, kv]], axis=0)
        for p in range(A_GROUP // 2):
            h0 = g * A_GROUP + 2 * p
            qcols = slice((h0 // 2) * LANES, (h0 // 2 + 1) * LANES)
            q2 = q_ref[:, qcols] * (A_HEAD_DIM ** -0.5)
            lhs = jnp.concatenate([q2 * mlo, q2 * mhi], axis=0)
            s = lax.dot_general(lhs, kband, (((1,), (1,)), ((), ())),
                                preferred_element_type=F32)
            slope = jnp.where(top, slope_ref[h0], slope_ref[h0 + 1])
            sink = jnp.where(top_col, sink_ref[h0], sink_ref[h0 + 1])
            s = s - slope * distf
            s = jnp.where(valid, s, NEG)
            m = jnp.maximum(jnp.max(s, axis=-1, keepdims=True), sink)
            pexp = jnp.exp(s - m)
            denom = jnp.sum(pexp, axis=-1, keepdims=True) + jnp.exp(sink - m)
            probs = (pexp / denom).astype(BF16)
            o2 = jnp.dot(probs, vband, preferred_element_type=F32)
            o_ref[:, qcols] = jnp.where(lo_f, o2[:blk], o2[blk:]).astype(o_ref.dtype)


def _swa_attention(qkv, sinks):
    s = qkv.shape[0]
    nb = s // WINDOW
    prev = lambda i: jnp.maximum(i - 1, 0)
    smem = pl.BlockSpec(memory_space=pltpu.SMEM)
    kv_cols = A_KV_HEADS * A_PAIR_COLS
    k_blk, v_blk = A_Q_COLS // kv_cols, A_Q_COLS // kv_cols + 1
    return pl.pallas_call(
        _swa_kernel,
        grid=(nb,),
        in_specs=[pl.BlockSpec((WINDOW, A_Q_COLS), lambda i: (i, 0)),
                  pl.BlockSpec((WINDOW, kv_cols), lambda i: (prev(i), k_blk)),
                  pl.BlockSpec((WINDOW, kv_cols), lambda i: (i, k_blk)),
                  pl.BlockSpec((WINDOW, kv_cols), lambda i: (prev(i), v_blk)),
                  pl.BlockSpec((WINDOW, kv_cols), lambda i: (i, v_blk)),
                  smem, smem],
        out_specs=pl.BlockSpec((WINDOW, A_Q_COLS), lambda i: (i, 0)),
        out_shape=jax.ShapeDtypeStruct((s, A_Q_COLS), BF16),
        compiler_params=_params("parallel"),
        name="swa_attention",
    )(qkv, qkv, qkv, qkv, qkv, sinks.astype(F32), _alibi_slopes(A_HEADS))


B_QK_COLS = B_HEADS * 2 * B_HEAD_DIM
B_K_BLOCK0 = B_QK_COLS // B_V_DIM
B_V_BLOCK0 = 2 * B_K_BLOCK0


_ALIBI_COLS = 12
_POS_LOW = 256
_EXP_ZERO_BELOW = 110.0
_BOUND_SLACK = 1.001
_BOUND_MARGIN = 1.0


def _diff_head_tables(slope, lhs_scr, kx_scr, vx_scr, tq):
    sv = jnp.full((1, LANES), slope, F32)
    s1 = sv.astype(BF16).astype(F32)
    s2 = (sv - s1).astype(BF16).astype(F32)
    s3 = (sv - s1 - s2).astype(BF16).astype(F32)
    lane = lax.broadcasted_iota(jnp.int32, (1, LANES), 1)
    third = lambda k: (lane == k) | (lane == k + 3) | (lane == k + 6) | (lane == k + 9)
    piece = jnp.where(third(0), s1, jnp.where(third(1), s2, s3))

    row = lax.broadcasted_iota(jnp.int32, (2 * tq, LANES), 0)
    lane_q = lax.broadcasted_iota(jnp.int32, (2 * tq, LANES), 1)
    r = jnp.where(row >= tq, row - tq, row)
    r_lo = jnp.bitwise_and(r, _POS_LOW - 1).astype(F32)
    r_hi = jnp.where(r >= _POS_LOW, 1.0, 0.0)
    ext = jnp.where(lane_q < 3, r_lo, jnp.where(lane_q < 6, r_hi,
                    jnp.where(lane_q < _ALIBI_COLS, piece, 0.0)))
    lhs_scr[:, LANES:2 * LANES] = ext.astype(BF16)

    c = lax.broadcasted_iota(jnp.int32, (tq, LANES), 0)
    lane_k = lax.broadcasted_iota(jnp.int32, (tq, LANES), 1)
    c_lo = jnp.bitwise_and(c, _POS_LOW - 1).astype(F32)
    c_hi = jnp.where(c >= _POS_LOW, float(_POS_LOW), 0.0)
    kext = jnp.where(lane_k < 3, -piece, jnp.where(lane_k < 6, -float(_POS_LOW) * piece,
                     jnp.where(lane_k < 9, c_lo, jnp.where(lane_k < _ALIBI_COLS, c_hi, 0.0))))
    kx_scr[...] = kext.astype(BF16)
    vx_scr[...] = jnp.where(lane_k == 0, 1.0, 0.0).astype(BF16)


def _diff_kernel(q_ref, k_ref, v_ref, lq1_ref, lk1_ref, lq2_ref, lk2_ref, g_ref, slope_ref,
                 o_ref, lhs_scr, kx_scr, vx_scr, m_scr, acc_scr, s_scr, k2max_scr, *, tq,
                 lambda_init):
    h = pl.program_id(0)
    i = pl.program_id(1)
    slope = slope_ref[h]

    @pl.when(i == 0)
    def _head_init():
        _diff_head_tables(slope, lhs_scr, kx_scr, vx_scr, tq)
        k2 = jnp.square(k_ref[...].astype(F32))
        k2max = jnp.maximum(*[jnp.max(jnp.sum(k2 * msk, axis=-1, keepdims=True), axis=0,
                                      keepdims=True) for msk in _half_masks(F32)])
        k2max_scr[...] = jnp.broadcast_to(k2max, k2max_scr.shape)

    mlo, mhi = _half_masks(BF16)
    q = q_ref[...] * (B_HEAD_DIM ** -0.5)
    lhs_scr[0:tq, 0:LANES] = q * mlo
    lhs_scr[tq:2 * tq, 0:LANES] = q * mhi
    m_scr[...] = jnp.full(m_scr.shape, -jnp.inf, F32)
    acc_scr[...] = jnp.zeros(acc_scr.shape, F32)

    def scores(j, slot):
        start = pl.multiple_of(j * tq, tq)
        kx = jnp.concatenate([k_ref[pl.ds(start, tq), :], kx_scr[...]], axis=1)
        s_scr[slot] = lax.dot_general(lhs_scr[...], kx, (((1,), (1,)), ((), ())),
                                      preferred_element_type=F32)

    def accumulate(j, slot, diagonal):
        start = pl.multiple_of(j * tq, tq)
        vx = jnp.concatenate([v_ref[pl.ds(start, tq), :], vx_scr[...]], axis=1)
        for half in range(2):
            rows = slice(half * tq, (half + 1) * tq)
            s = s_scr[slot, rows, :]
            if diagonal:
                row = lax.broadcasted_iota(jnp.int32, (tq, tq), 0)
                col = lax.broadcasted_iota(jnp.int32, (tq, tq), 1)
                s = jnp.where(row >= col, s, NEG)
                off = 0.0
            else:
                off = slope * ((i - j) * tq).astype(F32)
            m_old = m_scr[rows, :]
            m_new = jnp.maximum(m_old, jnp.max(s, axis=-1, keepdims=True) - off)
            p = jnp.exp(s - (m_new + off))
            alpha = jnp.exp(m_old - m_new)
            acc_scr[rows, :] = alpha * acc_scr[rows, :] + jnp.dot(
                p.astype(BF16), vx, preferred_element_type=F32)
            m_scr[rows, :] = m_new

    scores(i, 0)
    scores(jnp.maximum(i - 1, 0), 1)
    accumulate(i, 0, True)
    qf = (q * mlo).astype(F32), (q * mhi).astype(F32)
    q2max = jnp.maximum(*[jnp.max(jnp.sum(c * c, axis=-1, keepdims=True), axis=0, keepdims=True)
                          for c in qf])
    m_min = jnp.min(m_scr[...], axis=0, keepdims=True)
    reach = (jnp.sqrt(q2max * k2max_scr[0:1, 0:1]) * _BOUND_SLACK + slope * (tq - 1)
             - m_min + _EXP_ZERO_BELOW + _BOUND_MARGIN) / (slope * tq)
    n_far = jnp.floor(jnp.minimum(reach, i.astype(F32))).astype(jnp.int32)[0, 0]

    def pair(jj, carry):
        j = i - 1 - 2 * jj
        scores(j - 1, 0)
        accumulate(j, 1, False)
        scores(jnp.maximum(j - 2, 0), 1)
        accumulate(j - 1, 0, False)
        return carry

    lax.fori_loop(0, jnp.right_shift(n_far, 1), pair, 0)

    @pl.when(jnp.bitwise_and(n_far, 1) == 1)
    def _odd_tail():
        accumulate(i - n_far, 1, False)

    lam = (jnp.exp(jnp.sum(lq1_ref[...] * lk1_ref[...], axis=-1, keepdims=True))
           - jnp.exp(jnp.sum(lq2_ref[...] * lk2_ref[...], axis=-1, keepdims=True))
           + lambda_init)
    o0 = acc_scr[0:tq, 0:B_V_DIM] / acc_scr[0:tq, B_V_DIM:B_V_DIM + 1]
    o1 = acc_scr[tq:2 * tq, 0:B_V_DIM] / acc_scr[tq:2 * tq, B_V_DIM:B_V_DIM + 1]
    o = o0 - lam * o1
    o = o * lax.rsqrt(jnp.mean(o * o, axis=-1, keepdims=True) + LN_EPS)
    o = o * g_ref[...] * (1.0 - lambda_init)
    o_ref[...] = o.astype(o_ref.dtype)


def _diff_attention(qkv, lq1, lk1, lq2, lk2, subln_g, lambda_init, tq=512):
    s = qkv.shape[0]
    tq = _tile(s, tq)
    assert tq <= 2 * _POS_LOW and tq % _POS_LOW == 0, tq
    vec = lambda n: pl.BlockSpec((1, n), lambda h, i: (0, 0))
    kern = functools.partial(_diff_kernel, tq=tq, lambda_init=lambda_init)
    return pl.pallas_call(
        kern,
        grid=(B_HEADS, s // tq),
        in_specs=[pl.BlockSpec((tq, B_V_DIM), lambda h, i: (i, h)),
                  pl.BlockSpec((s, B_V_DIM), lambda h, i: (0, B_K_BLOCK0 + h)),
                  pl.BlockSpec((s, B_V_DIM), lambda h, i: (0, B_V_BLOCK0 + h)),
                  vec(B_HEAD_DIM), vec(B_HEAD_DIM), vec(B_HEAD_DIM), vec(B_HEAD_DIM),
                  vec(B_V_DIM), pl.BlockSpec(memory_space=pltpu.SMEM)],
        out_specs=pl.BlockSpec((tq, B_V_DIM), lambda h, i: (i, h)),
        out_shape=jax.ShapeDtypeStruct((s, B_HEADS * B_V_DIM), BF16),
        scratch_shapes=[pltpu.VMEM((2 * tq, 2 * LANES), BF16), pltpu.VMEM((tq, LANES), BF16),
                        pltpu.VMEM((tq, LANES), BF16), pltpu.VMEM((2 * tq, 1), F32),
                        pltpu.VMEM((2 * tq, 2 * LANES), F32), pltpu.VMEM((2, 2 * tq, tq), F32),
                        pltpu.VMEM((8, LANES), F32)],
        compiler_params=_params("arbitrary", "arbitrary"),
        name="diff_attention",
    )(qkv, qkv, qkv, lq1.reshape(1, -1).astype(F32), lk1.reshape(1, -1).astype(F32),
      lq2.reshape(1, -1).astype(F32), lk2.reshape(1, -1).astype(F32),
      subln_g.reshape(1, -1).astype(F32), _alibi_slopes(B_HEADS))


_N_RANKED = P_TOPK + 1
_PAIR_RANKS = [(i, j) for i in range(_N_RANKED) for j in range(_N_RANKED)
               if (i + 1) * (j + 1) <= _N_RANKED]
SUBLANES = 8
_CAND_ROWS = 64
assert len(_PAIR_RANKS) <= _CAND_ROWS


def _sorting_network(n):
    pairs = []
    p = 1
    while p < n:
        k = p
        while k >= 1:
            for j in range(k % p, n - k, 2 * k):
                for i in range(min(k, n - j - k)):
                    if (i + j) // (2 * p) == (i + j + k) // (2 * p):
                        pairs.append((i + j, i + j + k))
            k //= 2
        p *= 2
    return pairs


def _top_values(work, count):
    n = work.shape[0] // SUBLANES
    r = [work[SUBLANES * k:SUBLANES * (k + 1), :] for k in range(n)]
    for lo, hi in _sorting_network(n):
        r[lo], r[hi] = jnp.maximum(r[lo], r[hi]), jnp.minimum(r[lo], r[hi])
    out = []
    for k in range(count):
        top = jnp.max(r[0], axis=0, keepdims=True)
        out.append(top)
        needed = count - 1 - k
        hit = r[0] == top
        for i in range(min(needed, n - 1)):
            r[i] = jnp.where(hit, r[i + 1], r[i])
        if needed >= n:
            r[n - 1] = jnp.where(hit, -jnp.inf, r[n - 1])
    return out


def _peer_score_kernel(x_ref, wq_ref, sk_ref, thr_ref, e1_ref, a2_ref, e2_ref, cand_scr, q_ref):
    tt = x_ref.shape[0]
    q_ref[...] = jnp.dot(x_ref[...], wq_ref[...], preferred_element_type=F32).astype(q_ref.dtype)
    cand_scr[...] = jnp.full(cand_scr.shape, -jnp.inf, F32)
    for h in range(P_HEADS):
        st = []
        for c in range(2):
            qs = q_ref[:, (2 * h + c) * P_HALF:(2 * h + c + 1) * P_HALF]
            st.append(lax.dot_general(sk_ref[h, c], qs, (((1,), (1,)), ((), ())),
                                      preferred_element_type=F32))
        a = _top_values(st[0], _N_RANKED)
        b = _top_values(st[1], _N_RANKED)
        for r, (i, j) in enumerate(_PAIR_RANKS):
            cand_scr[r:r + 1, :] = a[i] + b[j]
        best = _top_values(cand_scr[...], _N_RANKED)
        z = jnp.zeros((1, tt), F32)
        for v in best[:P_TOPK]:
            z = z + jnp.exp(v - best[0])
        cut = 0.5 * (best[P_TOPK - 1] + best[P_TOPK])
        thr_ref[h] = cut - st[0]
        a2_ref[h] = st[1]
        e1_ref[h] = jnp.exp(st[0] - a[0]) / z
        e2_ref[h] = jnp.exp(st[1] - b[0])


def _peer_scores(x_bf, w_q, subkeys, layer, tt=512):
    s, d = x_bf.shape
    dq = w_q.shape[2]
    tt = _tile(s, tt)
    big = jax.ShapeDtypeStruct((P_HEADS, N_KEYS, s), F32)
    big_spec = pl.BlockSpec((P_HEADS, N_KEYS, tt), lambda t: (0, 0, t))
    return pl.pallas_call(
        _peer_score_kernel,
        grid=(s // tt,),
        in_specs=[pl.BlockSpec((tt, d), lambda t: (t, 0)),
                  pl.BlockSpec((None, d, dq), lambda t: (layer, 0, 0)),
                  pl.BlockSpec((None,) + subkeys.shape[1:], lambda t: (layer, 0, 0, 0, 0))],
        out_specs=[big_spec, big_spec, big_spec, big_spec],
        out_shape=[big, big, big, big],
        scratch_shapes=[pltpu.VMEM((_CAND_ROWS, tt), F32), pltpu.VMEM((tt, dq), BF16)],
        compiler_params=_params("parallel"),
        name="peer_scores",
    )(x_bf, w_q, subkeys)


def _gelu(a):
    return 0.5 * a * (1.0 + lax.erf(a * np.float32(np.sqrt(0.5))))


def _peer_dense_kernel(x_ref, u_ref, v_ref, thr_ref, e1_ref, a2_ref, e2_ref, xres_ref, g_ref,
                       b_ref, of_ref, ob_ref, act_scr, h_scr, acc_scr, *, n_etiles):
    s = pl.program_id(0)
    _, te, tt = act_scr.shape
    e_prev = jnp.maximum(s - 1, 0) % n_etiles

    @pl.when(s == 0)
    def _first():
        act_scr[1] = jnp.zeros((te, tt), F32)

    @pl.when(e_prev == 0)
    def _init():
        acc_scr[...] = jnp.zeros(acc_scr.shape, F32)

    def body(write_slot, read_slot):
        act_scr[write_slot] = lax.dot_general(u_ref[...], x_ref[...], (((1,), (1,)), ((), ())),
                                              preferred_element_type=F32)
        for il in range(te // N_KEYS):
            rows = slice(il * N_KEYS, (il + 1) * N_KEYS)
            for lg in range(tt // LANES):
                cols = slice(lg * LANES, (lg + 1) * LANES)
                gate = jnp.zeros((N_KEYS, LANES), F32)
                for h in range(P_HEADS):
                    w = e2_ref[h, :, cols] * e1_ref[h, il:il + 1, cols]
                    gate = jnp.where(a2_ref[h, :, cols] > thr_ref[h, il:il + 1, cols],
                                     gate + w, gate)
                h_scr[rows, cols] = (gate * _gelu(act_scr[read_slot, rows, cols])).astype(BF16)
        acc_scr[...] += lax.dot_general(h_scr[...], v_ref[...], (((0,), (0,)), ((), ())),
                                        preferred_element_type=F32)

    @pl.when(s % 2 == 0)
    def _even():
        body(0, 1)

    @pl.when(s % 2 == 1)
    def _odd():
        body(1, 0)

    @pl.when((e_prev == n_etiles - 1) & (s > 0))
    def _finish():
        out = _layer_norm_rows(ALPHA * xres_ref[...] + acc_scr[...], g_ref[...], b_ref[...])
        of_ref[...] = out
        ob_ref[...] = out.astype(BF16)


def _peer_dense_ln(x_bf, u_bf, v_bf, layer, thr, e1, a2, e2, x_res, g, b, tt=512, te=1024):
    s, d = x_bf.shape
    n_exp = u_bf.shape[1]
    tt, te = _tile(s, tt), _tile(n_exp, te)
    ne = n_exp // te
    n_items = (s // tt) * ne
    cur = lambda i: jnp.minimum(i, n_items - 1)
    prev = lambda i: jnp.maximum(i - 1, 0)
    rows_spec = pl.BlockSpec((P_HEADS, te // N_KEYS, tt),
                             lambda i: (0, prev(i) % ne, prev(i) // ne))
    once = pl.Buffered(1)
    full_spec = pl.BlockSpec((P_HEADS, N_KEYS, tt), lambda i: (0, 0, prev(i) // ne),
                             pipeline_mode=once)
    tok_spec = pl.BlockSpec((tt, d), lambda i: (prev(i) // ne, 0))
    res_spec = pl.BlockSpec((tt, d), lambda i: (prev(i) // ne, 0), pipeline_mode=once)
    vec_spec = pl.BlockSpec((1, d), lambda i: (0, 0))
    return pl.pallas_call(
        functools.partial(_peer_dense_kernel, n_etiles=ne),
        grid=(n_items + 1,),
        in_specs=[pl.BlockSpec((tt, d), lambda i: (cur(i) // ne, 0)),
                  pl.BlockSpec((None, te, d), lambda i: (layer, cur(i) % ne, 0)),
                  pl.BlockSpec((None, te, d), lambda i: (layer, prev(i) % ne, 0)),
                  rows_spec, rows_spec, full_spec, full_spec, res_spec, vec_spec, vec_spec],
        out_specs=[tok_spec, tok_spec],
        out_shape=[jax.ShapeDtypeStruct((s, d), F32), jax.ShapeDtypeStruct((s, d), BF16)],
        scratch_shapes=[pltpu.VMEM((2, te, tt), F32), pltpu.VMEM((te, tt), BF16),
                        pltpu.VMEM((tt, d), F32)],
        compiler_params=_params("arbitrary"),
        name="peer_dense",
    )(x_bf, u_bf, v_bf, thr, e1, a2, e2, x_res, g.reshape(1, d), b.reshape(1, d))


def _peer_ffn_ln(x_f, x_bf, wq_bf, subkeys_bf, u_bf, v_bf, layer, g, b):
    thr, e1, a2, e2 = _peer_scores(x_bf, wq_bf, subkeys_bf, layer)
    return _peer_dense_ln(x_bf, u_bf, v_bf, layer, thr, e1, a2, e2, x_f, g, b)


def kernel(x, a_w_qkv, a_sinks, a_w_o, b_w_qkv, b_lambda_q1, b_lambda_k1, b_lambda_q2,
           b_lambda_k2, b_subln_g, b_w_o, ln1_g, ln1_b, ln2_g, ln2_b,
           peer_w_q, peer_subkeys, peer_u, peer_v):
    bsz, seq, d = x.shape
    xf = x.reshape(bsz * seq, d).astype(F32)
    assert bsz == 1, "attention kernels index one sequence"
    xb = xf.astype(BF16)
    u_bf, v_bf = peer_u.astype(BF16), peer_v.astype(BF16)
    wq_bf, sk_bf = peer_w_q.astype(BF16), peer_subkeys.astype(BF16)
    a_qkv_bf, a_o_bf = _swa_weight(a_w_qkv), a_w_o.astype(BF16)
    b_qkv_bf, b_o_bf = b_w_qkv.astype(BF16), b_w_o.astype(BF16)
    for i in range(DEPTH):
        j = i // 2
        if i % 2 == 0:
            qkv = _matmul(xb, a_qkv_bf, j, BF16)
            mix = _swa_attention(qkv, a_sinks[j])
            w_o = a_o_bf
        else:
            lambda_init = 0.8 - 0.6 * float(np.exp(-0.3 * i))
            qkv = _matmul(xb, b_qkv_bf, j, BF16)
            mix = _diff_attention(qkv, b_lambda_q1[j], b_lambda_k1[j], b_lambda_q2[j],
                                  b_lambda_k2[j], b_subln_g[j], lambda_init)
            w_o = b_o_bf
        xf, xb = _proj_residual_ln(mix, w_o, j, xf, ln1_g[i], ln1_b[i])
        xf, xb = _peer_ffn_ln(xf, xb, wq_bf, sk_bf, u_bf, v_bf, i, ln2_g[i], ln2_b[i])
    return xf.reshape(bsz, seq, d).astype(x.dtype)
```
